```python
import math
import jax, jax.numpy as jnp
from jax import lax
import numpy as np

D_MODEL = 1024
BATCH = 8
SEQ = 4096
DEPTH = 2

GRID_W = 64
CTX_LEN = 256
N_MOD = 6
NORM_EPS = 1e-6
ROPE_THETA = 10000.0
Q_BLOCK = 128

HY_W = 256
HY_ORDER = 2
HY_BANDS = 16
HY_EMB = 1 + 2 * HY_BANDS
HY_FILT = 64
HY_CONV = 3
HY_TARGET = 1e-2
HY_FAST_PCT = 0.3
HY_SLOW_PCT = 1.5

GQA_HEADS = 6
GQA_KV_HEADS = 2
GQA_HEAD_DIM = 64

MLA_HEADS = 6
MLA_Q_RANK = 256
MLA_KV_RANK = 128
MLA_NOPE_DIM = 64
MLA_ROPE_DIM = 32
MLA_V_DIM = 64
MLA_QK_DIM = MLA_NOPE_DIM + MLA_ROPE_DIM

MIX_WIDTH = HY_W + GQA_HEADS * GQA_HEAD_DIM + MLA_HEADS * MLA_V_DIM
IN_SPLITS = (HY_W * (HY_ORDER + 1), GQA_HEADS * GQA_HEAD_DIM, GQA_KV_HEADS * GQA_HEAD_DIM,
             GQA_KV_HEADS * GQA_HEAD_DIM, MLA_Q_RANK, MLA_KV_RANK, MLA_ROPE_DIM)
IN_WIDTH = 1824
FFN_HIDDEN = ((8 * D_MODEL + 3 * 256 - 1) // (3 * 256)) * 256

kernel_name = "hybrid_hyena_gqa_mla_dit_block"


def rmsnorm(x, g):
    x32 = x.astype(jnp.float32)
    y = x32 * lax.rsqrt(jnp.mean(x32 * x32, axis=-1, keepdims=True) + NORM_EPS)
    return (y * g.astype(jnp.float32)).astype(x.dtype)


def split_cols(u, sizes):
    return jnp.split(u, np.cumsum(sizes)[:-1].tolist(), axis=-1)


def short_conv(u, w, b):
    up = jnp.pad(u, ((0, 0), (1, 1), (0, 0)))
    return up[:, :-2] * w[0] + up[:, 1:-1] * w[1] + up[:, 2:] * w[2] + b


def hyena_kernels(L, w1, b1, w2, b2, w3, freq):
    t = jnp.linspace(0.0, 1.0, L, dtype=jnp.float32)[:, None]
    wpos = 2.0 * math.pi * jnp.arange(L, dtype=jnp.float32)[:, None] / L
    f = jnp.linspace(1e-4, HY_BANDS - 1, HY_BANDS, dtype=jnp.float32)[None, :]
    z = jnp.concatenate([t, jnp.cos(f * wpos), -jnp.sin(f * wpos)], axis=-1)
    h = jnp.sin(freq * (z @ w1 + b1))
    h = jnp.sin(freq * (h @ w2 + b2))
    h = (h @ w3).astype(jnp.float32).reshape(L, HY_ORDER, 2, HY_W)
    deltas = jnp.abs(jnp.linspace(math.log(HY_TARGET) / HY_FAST_PCT, math.log(HY_TARGET) / HY_SLOW_PCT,
                                  HY_W, dtype=jnp.float32))
    h = h * jnp.exp(-t * deltas)[:, None, None, :]
    fwd, bwd = h[:, :, 0], h[:, :, 1]
    k = jnp.concatenate([fwd, jnp.zeros_like(fwd[:1]), bwd[:0:-1]], axis=0)
    return k / jnp.sum(jnp.abs(k), axis=0, keepdims=True)


def long_conv(z, k, bias):
    L = z.shape[1]
    z32 = z.astype(jnp.float32)
    Z = jnp.fft.rfft(z32, n=2 * L, axis=1)
    K = jnp.fft.rfft(k, n=2 * L, axis=0)
    y = jnp.fft.irfft(Z * K[None], n=2 * L, axis=1)[:, :L]
    return (y + z32 * bias.astype(jnp.float32)).astype(z.dtype)


def hyena_mixer(u, lp):
    L = u.shape[1]
    u = short_conv(u, lp["hy_conv_w"], lp["hy_conv_b"])
    v, x1, x2 = jnp.split(u, 3, axis=-1)
    k = hyena_kernels(L, lp["hy_filt_w1"], lp["hy_filt_b1"], lp["hy_filt_w2"], lp["hy_filt_b2"],
                      lp["hy_filt_w3"], lp["hy_filt_freq"])
    z = x1 * long_conv(v, k[:, 0], lp["hy_bias"][0])
    return x2 * long_conv(z, k[:, 1], lp["hy_bias"][1])


def _rope_1d(x, pos):
    d = x.shape[-1]
    inv = ROPE_THETA ** (-jnp.arange(0, d, 2, dtype=jnp.float32) / d)
    ang = pos.astype(jnp.float32)[:, None] * inv[None, :]
    ang = jnp.concatenate([ang, ang], axis=-1)[None, :, None, :]
    x32 = x.astype(jnp.float32)
    x1, x2 = jnp.split(x32, 2, axis=-1)
    rot = jnp.concatenate([-x2, x1], axis=-1)
    return (x32 * jnp.cos(ang) + rot * jnp.sin(ang)).astype(x.dtype)


def rope_2d(x, row, col):
    xr, xc = jnp.split(x, 2, axis=-1)
    return jnp.concatenate([_rope_1d(xr, row), _rope_1d(xc, col)], axis=-1)


def attn_queries(gq, mq, lp, row, col):
    B, L, _ = gq.shape
    qg = rmsnorm(gq.reshape(B, L, GQA_HEADS, GQA_HEAD_DIM), lp["gqa_q_g"])
    qm = (rmsnorm(mq, lp["mla_q_g"]) @ lp["mla_w_uq"]).reshape(B, L, MLA_HEADS, MLA_QK_DIM)
    qm_nope, qm_pe = jnp.split(qm, [MLA_NOPE_DIM], axis=-1)
    if row is not None:
        qg = rope_2d(qg, row, col)
        qm_pe = rope_2d(qm_pe, row, col)
    qm = jnp.concatenate([qm_nope, qm_pe], axis=-1)
    qg = qg.reshape(B, L, GQA_KV_HEADS, GQA_HEADS // GQA_KV_HEADS, GQA_HEAD_DIM)
    return qg, qm[:, :, :, None, :]


def attn_keys_values(gk, gv, mkv, mkr, lp, row, col):
    B, L, _ = gk.shape
    kg = rmsnorm(gk.reshape(B, L, GQA_KV_HEADS, GQA_HEAD_DIM), lp["gqa_k_g"])
    vg = gv.reshape(B, L, GQA_KV_HEADS, GQA_HEAD_DIM)
    kv = (rmsnorm(mkv, lp["mla_kv_g"]) @ lp["mla_w_ukv"]).reshape(B, L, MLA_HEADS, MLA_NOPE_DIM + MLA_V_DIM)
    km_nope, vm = jnp.split(kv, [MLA_NOPE_DIM], axis=-1)
    km_pe = mkr[:, :, None, :]
    if row is not None:
        kg = rope_2d(kg, row, col)
        km_pe = rope_2d(km_pe, row, col)
    km = jnp.concatenate([km_nope, jnp.broadcast_to(km_pe, (B, L, MLA_HEADS, MLA_ROPE_DIM))], axis=-1)
    return kg, vg, km, vm


def blocked_attention(q, k, v, scale):
    B, L, KH, G, Dq = q.shape
    nb = L // Q_BLOCK
    qb = jnp.moveaxis(q.reshape(B, nb, Q_BLOCK, KH, G, Dq), 1, 0)
    k32 = k.astype(jnp.float32)
    v32 = v.astype(jnp.float32)

    def one_block(qblk):
        s = jnp.einsum("bqkgd,bskd->bkgqs", qblk.astype(jnp.float32), k32) * scale
        p = jax.nn.softmax(s, axis=-1)
        return jnp.einsum("bkgqs,bskd->bqkgd", p, v32).astype(v.dtype)

    o = lax.map(one_block, qb)
    return jnp.moveaxis(o, 0, 1).reshape(B, L, KH * G * v.shape[-1])


def attend(q, kv):
    qg, qm = q
    kg, vg, km, vm = kv
    yg = blocked_attention(qg, kg, vg, GQA_HEAD_DIM ** -0.5)
    ym = blocked_attention(qm, km, vm, MLA_QK_DIM ** -0.5)
    return yg, ym


def merge_heads(y_hy, y_att, w_out):
    return jnp.concatenate([y_hy, y_att[0], y_att[1]], axis=-1) @ w_out


def swiglu(h, w1, w3, w2):
    return (jax.nn.silu(h @ w1) * (h @ w3)) @ w2


def setup_inputs(seed: int = 0) -> dict:
    key = jax.random.key(seed)
    ks = iter(jax.random.split(key, 32))

    def nrm(shape, scale):
        return jax.random.normal(next(ks), shape, jnp.float32) * scale

    def gain(shape):
        return 1.0 + nrm(shape, 0.05)

    D = D_MODEL
    return {
        "x": nrm((BATCH, SEQ, D), 1.0),
        "c": nrm((BATCH, D), 1.0),
        "ctx": nrm((BATCH, CTX_LEN, D), 1.0),
        "c_ctx": nrm((D,), 1.0),
        "mod_w": nrm((DEPTH, D, N_MOD * D), 0.5 * D ** -0.5),
        "mod_b": nrm((DEPTH, N_MOD * D), 0.01),
        "norm1_g": gain((DEPTH, D)),
        "norm2_g": gain((DEPTH, D)),
        "w_in": nrm((DEPTH, D, IN_WIDTH), D ** -0.5),
        "hy_conv_w": nrm((DEPTH, HY_CONV, HY_W * (HY_ORDER + 1)), HY_CONV ** -0.5),
        "hy_conv_b": nrm((DEPTH, HY_W * (HY_ORDER + 1)), 0.01),
        "hy_filt_w1": nrm((DEPTH, HY_EMB, HY_FILT), HY_EMB ** -0.5),
        "hy_filt_b1": nrm((DEPTH, HY_FILT), 0.1),
        "hy_filt_w2": nrm((DEPTH, HY_FILT, HY_FILT), HY_FILT ** -0.5),
        "hy_filt_b2": nrm((DEPTH, HY_FILT), 0.1),
        "hy_filt_w3": nrm((DEPTH, HY_FILT, HY_ORDER * 2 * HY_W), HY_FILT ** -0.5),
        "hy_filt_freq": gain((DEPTH, HY_FILT)),
        "hy_bias": nrm((DEPTH, HY_ORDER, HY_W), 0.5),
        "gqa_q_g": gain((DEPTH, GQA_HEAD_DIM)),
        "gqa_k_g": gain((DEPTH, GQA_HEAD_DIM)),
        "mla_q_g": gain((DEPTH, MLA_Q_RANK)),
        "mla_kv_g": gain((DEPTH, MLA_KV_RANK)),
        "mla_w_uq": nrm((DEPTH, MLA_Q_RANK, MLA_HEADS * MLA_QK_DIM), MLA_Q_RANK ** -0.5),
        "mla_w_ukv": nrm((DEPTH, MLA_KV_RANK, MLA_HEADS * (MLA_NOPE_DIM + MLA_V_DIM)), MLA_KV_RANK ** -0.5),
        "w_out": nrm((DEPTH, MIX_WIDTH, D), MIX_WIDTH ** -0.5),
        "ffn_w1": nrm((DEPTH, D, FFN_HIDDEN), D ** -0.5),
        "ffn_w3": nrm((DEPTH, D, FFN_HIDDEN), D ** -0.5),
        "ffn_w2": nrm((DEPTH, FFN_HIDDEN, D), FFN_HIDDEN ** -0.5),
        "final_g": gain((D,)),
    }


def reference(x, c, ctx, c_ctx, mod_w, mod_b, norm1_g, norm2_g, w_in, hy_conv_w, hy_conv_b,
              hy_filt_w1, hy_filt_b1, hy_filt_w2, hy_filt_b2, hy_filt_w3, hy_filt_freq, hy_bias,
              gqa_q_g, gqa_k_g, mla_q_g, mla_kv_g, mla_w_uq, mla_w_ukv, w_out,
              ffn_w1, ffn_w3, ffn_w2, final_g):
    B, n_lat, D = x.shape
    n_rows = n_lat // GRID_W
    row = jnp.repeat(jnp.arange(n_rows, dtype=jnp.int32), GRID_W)
    col = jnp.tile(jnp.arange(GRID_W, dtype=jnp.int32), n_rows)
    off = np.cumsum((0,) + IN_SPLITS).tolist()

    x_lat, x_ctx = x, ctx
    sc = jax.nn.silu(c)
    sc_ctx = jax.nn.silu(c_ctx)
    for l in range(DEPTH):
        last = l == DEPTH - 1
        lp = {
            "hy_conv_w": hy_conv_w[l], "hy_conv_b": hy_conv_b[l],
            "hy_filt_w1": hy_filt_w1[l], "hy_filt_b1": hy_filt_b1[l],
            "hy_filt_w2": hy_filt_w2[l], "hy_filt_b2": hy_filt_b2[l],
            "hy_filt_w3": hy_filt_w3[l], "hy_filt_freq": hy_filt_freq[l], "hy_bias": hy_bias[l],
            "gqa_q_g": gqa_q_g[l], "gqa_k_g": gqa_k_g[l],
            "mla_q_g": mla_q_g[l], "mla_kv_g": mla_kv_g[l],
            "mla_w_uq": mla_w_uq[l], "mla_w_ukv": mla_w_ukv[l],
        }
        w_in_l, w_out_l = w_in[l], w_out[l]

        mod = (sc @ mod_w[l] + mod_b[l]).reshape(B, N_MOD, D)
        shift1, scale1, gate1, shift2, scale2, gate2 = [mod[:, i, None, :] for i in range(N_MOD)]
        n_ctx_mod = 2 if last else N_MOD
        mod_c = (sc_ctx @ mod_w[l][:, :n_ctx_mod * D] + mod_b[l][:n_ctx_mod * D]).reshape(n_ctx_mod, D)

        h_c = rmsnorm(x_ctx, norm1_g[l]) * (1.0 + mod_c[1]) + mod_c[0]
        if last:
            w_kv = jnp.concatenate([w_in_l[:, off[2]:off[4]], w_in_l[:, off[5]:off[7]]], axis=1)
            gk_c, gv_c, mkv_c, mkr_c = split_cols(h_c @ w_kv, IN_SPLITS[2:4] + IN_SPLITS[5:7])
            kv_c = attn_keys_values(gk_c, gv_c, mkv_c, mkr_c, lp, None, None)
        else:
            hy_c, gq_c, gk_c, gv_c, mq_c, mkv_c, mkr_c = split_cols(h_c @ w_in_l, IN_SPLITS)
            kv_c = attn_keys_values(gk_c, gv_c, mkv_c, mkr_c, lp, None, None)
            q_c = attn_queries(gq_c, mq_c, lp, None, None)
            y_c = merge_heads(hyena_mixer(hy_c, lp), attend(q_c, kv_c), w_out_l)
            x_ctx_mid = x_ctx + mod_c[2] * y_c

        h = rmsnorm(x_lat, norm1_g[l]) * (1.0 + scale1) + shift1
        hy, gq, gk, gv, mq, mkv, mkr = split_cols(h @ w_in_l, IN_SPLITS)
        q = attn_queries(gq, mq, lp, row, col)
        kv = attn_keys_values(gk, gv, mkv, mkr, lp, row, col)
        kv_all = tuple(jnp.concatenate([a_c, a_l], axis=1) for a_c, a_l in zip(kv_c, kv))
        y = merge_heads(hyena_mixer(hy, lp), attend(q, kv_all), w_out_l)
        x_lat = x_lat + gate1 * y

        h2 = rmsnorm(x_lat, norm2_g[l]) * (1.0 + scale2) + shift2
        x_lat = x_lat + gate2 * swiglu(h2, ffn_w1[l], ffn_w3[l], ffn_w2[l])
        if not last:
            h2_c = rmsnorm(x_ctx_mid, norm2_g[l]) * (1.0 + mod_c[4]) + mod_c[3]
            x_ctx = x_ctx_mid + mod_c[5] * swiglu(h2_c, ffn_w1[l], ffn_w3[l], ffn_w2[l])

    return rmsnorm(x_lat, final_g)
```

```python
import functools
import math

import jax
import jax.numpy as jnp
import numpy as np
from jax import lax
from jax.experimental import pallas as pl
from jax.experimental.pallas import tpu as pltpu

F32 = jnp.float32
BF16 = jnp.bfloat16

N_MOD = 6
GRID_W = 64
NORM_EPS = 1e-6
ROPE_THETA = 10000.0
HY_W = 256
HY_ORDER = 2
HY_BANDS = 16
HY_TARGET = 1e-2
HY_FAST_PCT = 0.3
HY_SLOW_PCT = 1.5
GQA_HEADS = 6
GQA_KV_HEADS = 2
GQA_HEAD_DIM = 64
MLA_HEADS = 6
MLA_NOPE_DIM = 64
MLA_ROPE_DIM = 32
MLA_V_DIM = 64
MLA_QK_DIM = MLA_NOPE_DIM + MLA_ROPE_DIM
IN_SPLITS = (768, 384, 128, 128, 256, 128, 32)
IN_OFF = tuple(int(v) for v in np.cumsum((0,) + IN_SPLITS))
IN_WIDTH_PAD = 1920

LANES = 128
VMEM_LIMIT = 56 * 1024 * 1024
CONV_BLK = 256

LOG2E = 1.4426950408889634
HIGHEST = lax.Precision.HIGHEST


def _cparams(*sem):
    return pltpu.CompilerParams(dimension_semantics=sem, vmem_limit_bytes=VMEM_LIMIT)


def _const_spec(shape):
    zeros = (0,) * len(shape)
    return pl.BlockSpec(shape, lambda *_: zeros, pipeline_mode=pl.Buffered(1))


def _mod_kernel(c_ref, w_ref, b_ref, o_ref):
    c = c_ref[...]
    sc = c * jax.nn.sigmoid(c)
    o_ref[0] = jnp.dot(sc, w_ref[0], precision=HIGHEST, preferred_element_type=F32) + b_ref[0]


def _modulation(c_all, mod_w, mod_b):
    depth, d, n = mod_w.shape
    rows = c_all.shape[0]
    tn = 1536
    return pl.pallas_call(
        _mod_kernel,
        grid=(depth, n // tn),
        in_specs=[
            pl.BlockSpec((rows, d), lambda l, j: (0, 0)),
            pl.BlockSpec((1, d, tn), lambda l, j: (l, 0, j)),
            pl.BlockSpec((1, 1, tn), lambda l, j: (l, 0, j)),
        ],
        out_specs=pl.BlockSpec((1, rows, tn), lambda l, j: (l, 0, j)),
        out_shape=jax.ShapeDtypeStruct((depth, rows, n), F32),
        compiler_params=_cparams("parallel", "parallel"),
        name="modulation",
    )(c_all, mod_w, mod_b.reshape(depth, 1, n))


def _filter_kernel(z_ref, w1_ref, b1_ref, w2_ref, b2_ref, fr_ref, w3f_ref, w3b_ref,
                   tcol_ref, sel_ref, delta_ref, o_ref):
    fr = fr_ref[...]
    h = jnp.sin(fr * (jnp.dot(z_ref[...], w1_ref[...], precision=HIGHEST,
                              preferred_element_type=F32) + b1_ref[...]))
    h = jnp.sin(fr * (jnp.dot(h, w2_ref[...], precision=HIGHEST,
                              preferred_element_type=F32) + b2_ref[...]))
    fwd = jnp.dot(h, w3f_ref[...], precision=HIGHEST, preferred_element_type=F32)
    bwd = jnp.dot(h, w3b_ref[...], precision=HIGHEST, preferred_element_type=F32)
    sel = sel_ref[...]
    k = jnp.where(sel > 0.0, fwd, jnp.where(sel < 0.0, bwd, 0.0))
    k = k * jnp.exp(-tcol_ref[...] * delta_ref[...])
    nrm = jnp.sum(jnp.abs(k), axis=0, keepdims=True)
    o_ref[0] = (k / nrm).T


def _filter_features(L):
    t = jnp.linspace(0.0, 1.0, L, dtype=F32)[:, None]
    wpos = 2.0 * math.pi * jnp.arange(L, dtype=F32)[:, None] / L
    f = jnp.linspace(1e-4, HY_BANDS - 1, HY_BANDS, dtype=F32)[None, :]
    z = jnp.concatenate([t, jnp.cos(f * wpos), -jnp.sin(f * wpos)], axis=-1)
    n = np.arange(2 * L)
    lag = np.where(n <= L, np.minimum(n, L - 1), 2 * L - n)
    sel = np.where(n < L, 1.0, np.where(n > L, -1.0, 0.0)).astype(np.float32)
    zf = jnp.pad(z[lag], ((0, 0), (0, LANES - z.shape[1])))
    return zf, t[lag], jnp.asarray(sel)[:, None]


def _hyena_filters(L, w1, b1, w2, b2, w3, freq):
    P = 2 * L
    zf, tcol, sel = _filter_features(L)
    nf = w1.shape[1]
    w1p = jnp.pad(w1, ((0, LANES - w1.shape[0]), (0, 0)))
    deltas = jnp.abs(jnp.linspace(math.log(HY_TARGET) / HY_FAST_PCT, math.log(HY_TARGET) / HY_SLOW_PCT,
                                  HY_W, dtype=F32))[None, :]
    halves = HY_W // LANES
    cb = 2 * halves
    return pl.pallas_call(
        _filter_kernel,
        grid=(HY_ORDER, halves),
        in_specs=[
            pl.BlockSpec((P, LANES), lambda o, c: (0, 0)),
            pl.BlockSpec((LANES, nf), lambda o, c: (0, 0)),
            pl.BlockSpec((1, nf), lambda o, c: (0, 0)),
            pl.BlockSpec((nf, nf), lambda o, c: (0, 0)),
            pl.BlockSpec((1, nf), lambda o, c: (0, 0)),
            pl.BlockSpec((1, nf), lambda o, c: (0, 0)),
            pl.BlockSpec((nf, LANES), lambda o, c: (0, o * cb + c)),
            pl.BlockSpec((nf, LANES), lambda o, c: (0, o * cb + halves + c)),
            pl.BlockSpec((P, 1), lambda o, c: (0, 0)),
            pl.BlockSpec((P, 1), lambda o, c: (0, 0)),
            pl.BlockSpec((1, LANES), lambda o, c: (0, c)),
        ],
        out_specs=pl.BlockSpec((1, LANES, P), lambda o, c: (o, c, 0)),
        out_shape=jax.ShapeDtypeStruct((HY_ORDER, HY_W, P), F32),
        compiler_params=_cparams("parallel", "parallel"),
        name="hyena_filters",
    )(zf, w1p, b1[None, :], w2, b2[None, :], freq[None, :], w3, w3, tcol, sel, deltas)


def _rope(t, c_ref, s1_ref, s2_ref, sh):
    return (t * c_ref[...] + pltpu.roll(t, LANES - sh, 1) * s1_ref[...]
            + pltpu.roll(t, sh, 1) * s2_ref[...])


def _inproj_kernel(x_ref, mod_ref, g1_ref, w_ref, e2_ref, gq_ref, gk_ref, gmq_ref, gmkv_ref,
                   wq_ref, wkp_ref, wvp_ref,
                   cg_ref, s1g_ref, s2g_ref, cm_ref, s1m_ref, s2m_ref, ck_ref, s1k_ref, s2k_ref,
                   hy_ref, q_ref, k_ref, v_ref, qm_ref, km_ref, vm_ref, *, q_scale, qm_scale):
    x = x_ref[0]
    shift = mod_ref[0, 0:1, :]
    scale = mod_ref[0, 1:2, :]
    h = x * lax.rsqrt(jnp.mean(x * x, axis=-1, keepdims=True) + NORM_EPS) * g1_ref[...]
    h = h * (1.0 + scale) + shift
    u = jnp.dot(h.astype(BF16), w_ref[...], preferred_element_type=F32)

    hy_ref[0] = u[:, IN_OFF[0]:IN_OFF[1]]

    e2 = e2_ref[...]

    def head_norm(t, g):
        sq = t * t
        hi = sq.astype(BF16)
        lo = (sq - hi.astype(F32)).astype(BF16)
        ss = (jnp.dot(hi, e2, preferred_element_type=F32) + jnp.dot(lo, e2, preferred_element_type=F32))
        return t * lax.rsqrt(ss * (1.0 / GQA_HEAD_DIM) + NORM_EPS) * g

    for i in range(GQA_HEADS * GQA_HEAD_DIM // LANES):
        t = u[:, IN_OFF[1] + LANES * i: IN_OFF[1] + LANES * (i + 1)]
        t = _rope(head_norm(t, gq_ref[...]), cg_ref, s1g_ref, s2g_ref, GQA_HEAD_DIM // 4)
        q_ref[0, :, LANES * i: LANES * (i + 1)] = (t * q_scale).astype(BF16)
    t = u[:, IN_OFF[2]:IN_OFF[3]]
    k_ref[0] = _rope(head_norm(t, gk_ref[...]), cg_ref, s1g_ref, s2g_ref, GQA_HEAD_DIM // 4).astype(BF16)
    v_ref[0] = u[:, IN_OFF[3]:IN_OFF[4]].astype(BF16)

    mq = u[:, IN_OFF[4]:IN_OFF[5]]
    mqn = mq * lax.rsqrt(jnp.mean(mq * mq, axis=-1, keepdims=True) + NORM_EPS) * gmq_ref[...]
    qm = jnp.dot(mqn.astype(BF16), wq_ref[...], preferred_element_type=F32)
    for hd in range(MLA_HEADS):
        t = _rope(qm[:, LANES * hd: LANES * (hd + 1)], cm_ref, s1m_ref, s2m_ref, MLA_ROPE_DIM // 4)
        qm_ref[0, :, LANES * hd: LANES * (hd + 1)] = (t * qm_scale).astype(BF16)

    mkv = u[:, IN_OFF[5]:IN_OFF[6]]
    kvn = (mkv * lax.rsqrt(jnp.mean(mkv * mkv, axis=-1, keepdims=True) + NORM_EPS) * gmkv_ref[...]).astype(BF16)
    kpe = _rope(u[:, IN_OFF[6]:IN_OFF[6] + LANES], ck_ref, s1k_ref, s2k_ref, MLA_ROPE_DIM // 4).astype(BF16)
    kcat = jnp.concatenate([kvn, kpe], axis=1)
    km_ref[0] = jnp.dot(kcat, wkp_ref[...], preferred_element_type=F32).astype(BF16)
    vm_ref[0] = jnp.dot(kvn, wvp_ref[...], preferred_element_type=F32).astype(BF16)


def _inproj(x, mod, lw, tabs):
    B, L, D = x.shape
    tm = min(512, L)
    nh = MLA_HEADS * LANES
    tab_spec = pl.BlockSpec((tm, LANES), lambda i, b: (i, 0))
    tok = lambda w: pl.BlockSpec((1, tm, w), lambda i, b: (b, i, 0))
    kern = functools.partial(_inproj_kernel,
                             q_scale=GQA_HEAD_DIM ** -0.5 * LOG2E, qm_scale=MLA_QK_DIM ** -0.5 * LOG2E)
    consts = [lw["g1"], lw["w_in"], lw["e2"], lw["gq"], lw["gk"], lw["gmq"], lw["gmkv"],
              lw["wq"], lw["wkp"], lw["wvp"]]
    out_w = [(IN_SPLITS[0], F32), (IN_SPLITS[1], BF16), (IN_SPLITS[2], BF16), (IN_SPLITS[3], BF16),
             (nh, BF16), (nh, BF16), (nh, BF16)]
    return pl.pallas_call(
        kern,
        grid=(L // tm, B),
        in_specs=[tok(D), pl.BlockSpec((1, N_MOD, D), lambda i, b: (b, 0, 0))]
        + [_const_spec(a.shape) for a in consts] + [tab_spec] * 9,
        out_specs=[tok(w) for w, _ in out_w],
        out_shape=[jax.ShapeDtypeStruct((B, L, w), dt) for w, dt in out_w],
        compiler_params=_cparams("parallel", "parallel"),
        name="inproj",
    )(x, mod, *consts, *tabs)


def _shortconv_kernel(u_ref, w_ref, b_ref, o_ref):
    u = u_ref[0]
    L = u.shape[0]
    row = lax.broadcasted_iota(jnp.int32, u.shape, 0)
    prev = jnp.where(row == 0, 0.0, pltpu.roll(u, 1, 0))
    nxt = jnp.where(row == L - 1, 0.0, pltpu.roll(u, L - 1, 0))
    y = prev * w_ref[0, 0:1, :] + u * w_ref[0, 1:2, :] + nxt * w_ref[0, 2:3, :] + b_ref[0]
    o_ref[0, 0] = y.T


def _shortconv_t(hy, conv_w, conv_b):
    B, L, _ = hy.shape
    w = jnp.transpose(conv_w.reshape(3, 3, HY_W), (1, 0, 2))
    b = conv_b.reshape(3, 1, HY_W)
    return pl.pallas_call(
        _shortconv_kernel,
        grid=(B, 3),
        in_specs=[
            pl.BlockSpec((1, L, HY_W), lambda bb, p: (bb, 0, p)),
            pl.BlockSpec((1, 3, HY_W), lambda bb, p: (p, 0, 0)),
            pl.BlockSpec((1, 1, HY_W), lambda bb, p: (p, 0, 0)),
        ],
        out_specs=pl.BlockSpec((1, 1, HY_W, L), lambda bb, p: (p, bb, 0, 0)),
        out_shape=jax.ShapeDtypeStruct((3, B, HY_W, L), F32),
        compiler_params=_cparams("parallel", "parallel"),
        name="hyena_shortconv",
    )(hy, w, b)


def _longconv_kernel(bias_ref, kf_ref, u_ref, o_ref, s_ref, acc_ref, *, nblk, batch):
    c = pl.program_id(0)
    P = s_ref.shape[1]
    rows = batch

    def conv(zin, o):
        krow = kf_ref[o, 0]
        s_ref[...] = pltpu.roll(jnp.broadcast_to(krow, (CONV_BLK, P)), 0, 1,
                                stride=1, stride_axis=0).astype(BF16)
        zb = zin.astype(BF16)
        acc_ref[...] = jnp.zeros_like(acc_ref)
        for d in range(-(nblk - 1), nblk):
            j0 = max(0, -d)
            n = nblk - abs(d)
            i0 = j0 + d
            col = (d * CONV_BLK) % P
            acc_ref[rows * i0: rows * (i0 + n), :] += jnp.dot(
                zb[rows * j0: rows * (j0 + n), :], s_ref[:, col: col + CONV_BLK],
                preferred_element_type=F32)
        return acc_ref[...] + zin * bias_ref[o, c]

    v = u_ref[0, 0]
    z1 = u_ref[1, 0] * conv(v, 0)
    o_ref[0] = u_ref[2, 0] * conv(z1, 1)


def _longconv(ut, kf, hy_bias):
    _, C, R, _ = ut.shape
    P = kf.shape[-1]
    nblk = P // (2 * CONV_BLK)
    batch = R // nblk
    kern = functools.partial(_longconv_kernel, nblk=nblk, batch=batch)
    return pl.pallas_call(
        kern,
        grid=(C,),
        in_specs=[
            pl.BlockSpec(memory_space=pltpu.SMEM),
            pl.BlockSpec((HY_ORDER, 1, 1, P), lambda c: (0, c, 0, 0)),
            pl.BlockSpec((3, 1, R, CONV_BLK), lambda c: (0, c, 0, 0)),
        ],
        out_specs=pl.BlockSpec((1, R, CONV_BLK), lambda c: (c, 0, 0)),
        out_shape=jax.ShapeDtypeStruct((C, R, CONV_BLK), F32),
        scratch_shapes=[pltpu.VMEM((CONV_BLK, P), BF16), pltpu.VMEM((R, CONV_BLK), F32)],
        compiler_params=_cparams("parallel"),
        name="hyena_longconv",
    )(hy_bias, kf.reshape(HY_ORDER, C, 1, P), ut)


def _hyena(hy, lw, kf):
    B, L, _ = hy.shape
    nblk = L // CONV_BLK
    ut = _shortconv_t(hy, lw["conv_w"], lw["conv_b"])
    ut = jnp.transpose(ut.reshape(3, B, HY_W, nblk, CONV_BLK), (0, 2, 3, 1, 4))
    y = _longconv(ut.reshape(3, HY_W, nblk * B, CONV_BLK), kf, lw["hy_bias"])
    y = jnp.transpose(y.reshape(HY_W, nblk, B, CONV_BLK), (2, 0, 1, 3))
    return y.reshape(B, HY_W, L)


def _softmax_pv(q, k_ref, v_ref, s_ref, koff, tk):
    M = q.shape[0]
    T = k_ref.shape[1]
    m = jnp.full((M, LANES), -jnp.inf, F32)
    for c in range(T // tk):
        kc = k_ref[0, c * tk:(c + 1) * tk, koff:koff + LANES]
        s = lax.dot_general(q, kc, (((1,), (1,)), ((), ())), preferred_element_type=F32)
        s_ref[:, c * tk:(c + 1) * tk] = s
        for j in range(tk // LANES):
            m = jnp.maximum(m, s[:, j * LANES:(j + 1) * LANES])
    mrow = jnp.max(m, axis=-1, keepdims=True)
    l = jnp.zeros((M, LANES), F32)
    acc = jnp.zeros((M, LANES), F32)
    for c in range(T // tk):
        p = jnp.exp2(s_ref[:, c * tk:(c + 1) * tk] - mrow)
        for j in range(tk // LANES):
            l = l + p[:, j * LANES:(j + 1) * LANES]
        acc = acc + jnp.dot(p.astype(BF16), v_ref[0, c * tk:(c + 1) * tk, koff:koff + LANES],
                            preferred_element_type=F32)
    return acc * (1.0 / jnp.sum(l, axis=-1, keepdims=True))


def _gqa_kernel(q_ref, k_ref, v_ref, o_ref, s_ref, *, tk):
    tq = q_ref.shape[1]
    ntile = q_ref.shape[2] // LANES
    low = lax.broadcasted_iota(jnp.int32, (tq, LANES), 1) < GQA_HEAD_DIM
    zero = jnp.zeros((tq, LANES), BF16)
    qs = []
    for t in range(ntile):
        qt = q_ref[0, :, LANES * t: LANES * (t + 1)]
        qs.append(jnp.where(low, qt, zero))
        qs.append(jnp.where(low, zero, qt))
    o = _softmax_pv(jnp.concatenate(qs, axis=0), k_ref, v_ref, s_ref, 0, tk)
    for t in range(ntile):
        oa = o[(2 * t) * tq:(2 * t + 1) * tq]
        ob = o[(2 * t + 1) * tq:(2 * t + 2) * tq]
        o_ref[0, :, LANES * t: LANES * (t + 1)] = jnp.where(low, oa, ob).astype(BF16)


def _gqa_attention(q, k, v):
    B, L, W = q.shape
    T = k.shape[1]
    tq = min(128, L)
    tk = 256
    return pl.pallas_call(
        functools.partial(_gqa_kernel, tk=tk),
        grid=(B, L // tq),
        in_specs=[
            pl.BlockSpec((1, tq, W), lambda b, i: (b, i, 0)),
            pl.BlockSpec((1, T, LANES), lambda b, i: (b, 0, 0)),
            pl.BlockSpec((1, T, LANES), lambda b, i: (b, 0, 0)),
        ],
        out_specs=pl.BlockSpec((1, tq, W), lambda b, i: (b, i, 0)),
        out_shape=jax.ShapeDtypeStruct((B, L, W), BF16),
        scratch_shapes=[pltpu.VMEM((GQA_HEADS * tq, T), F32)],
        compiler_params=_cparams("parallel", "parallel"),
        name="gqa_attention",
    )(q, k, v)


def _mla_kernel(q_ref, k_ref, v_ref, o_ref, s_ref, *, tk):
    o = _softmax_pv(q_ref[0, :, 0:LANES], k_ref, v_ref, s_ref, 0, tk)
    o = o + _softmax_pv(q_ref[0, :, LANES:2 * LANES], k_ref, v_ref, s_ref, LANES, tk)
    o_ref[0] = o.astype(BF16)


def _mla_attention(q, k, v):
    B, L, W = q.shape
    T = k.shape[1]
    npair = W // (2 * LANES)
    tq = min(512, L)
    tk = 256
    return pl.pallas_call(
        functools.partial(_mla_kernel, tk=tk),
        grid=(B, npair, L // tq),
        in_specs=[
            pl.BlockSpec((1, tq, 2 * LANES), lambda b, t, i: (b, i, t)),
            pl.BlockSpec((1, T, 2 * LANES), lambda b, t, i: (b, 0, t)),
            pl.BlockSpec((1, T, 2 * LANES), lambda b, t, i: (b, 0, t)),
        ],
        out_specs=pl.BlockSpec((1, tq, LANES), lambda b, t, i: (b, i, t)),
        out_shape=jax.ShapeDtypeStruct((B, L, npair * LANES), BF16),
        scratch_shapes=[pltpu.VMEM((tq, T), F32)],
        compiler_params=_cparams("parallel", "parallel", "parallel"),
        name="mla_attention",
    )(q, k, v)


def _outffn_kernel(x_ref, yh_ref, yg_ref, ym_ref, mod_ref, g2_ref, wo_ref, w1_ref, w3_ref, w2_ref,
                   gf_ref, o_ref, *, hchunk, final):
    x = x_ref[0]
    mix = jnp.concatenate([yh_ref[0].T.astype(BF16), yg_ref[0], ym_ref[0]], axis=1)
    y = jnp.dot(mix, wo_ref[...], preferred_element_type=F32)
    x1 = x + mod_ref[0, 2:3, :] * y
    h = x1 * lax.rsqrt(jnp.mean(x1 * x1, axis=-1, keepdims=True) + NORM_EPS) * g2_ref[...]
    h = (h * (1.0 + mod_ref[0, 4:5, :]) + mod_ref[0, 3:4, :]).astype(BF16)
    f = jnp.zeros_like(x)
    hidden = w1_ref.shape[1]
    for c in range(hidden // hchunk):
        a = jnp.dot(h, w1_ref[:, c * hchunk:(c + 1) * hchunk], preferred_element_type=F32)
        b = jnp.dot(h, w3_ref[:, c * hchunk:(c + 1) * hchunk], preferred_element_type=F32)
        g = (a * jax.nn.sigmoid(a) * b).astype(BF16)
        f = f + jnp.dot(g, w2_ref[c * hchunk:(c + 1) * hchunk, :], preferred_element_type=F32)
    x2 = x1 + mod_ref[0, 5:6, :] * f
    if final:
        x2 = x2 * lax.rsqrt(jnp.mean(x2 * x2, axis=-1, keepdims=True) + NORM_EPS) * gf_ref[...]
    o_ref[0] = x2


def _outffn(x, yh_t, yg, ym, mod, lw, final_g, final):
    B, L, D = x.shape
    tm = min(512, L)
    hidden = lw["w1"].shape[1]
    consts = [lw["g2"], lw["w_out"], lw["w1"], lw["w3"], lw["w2"], final_g]
    kern = functools.partial(_outffn_kernel, hchunk=hidden // 2, final=final)
    tok = lambda w: pl.BlockSpec((1, tm, w), lambda b, i: (b, i, 0))
    return pl.pallas_call(
        kern,
        grid=(B, L // tm),
        in_specs=[tok(D), pl.BlockSpec((1, HY_W, tm), lambda b, i: (b, 0, i)),
                  tok(yg.shape[2]), tok(ym.shape[2]),
                  pl.BlockSpec((1, N_MOD, D), lambda b, i: (b, 0, 0))]
        + [_const_spec(a.shape) for a in consts],
        out_specs=tok(D),
        out_shape=jax.ShapeDtypeStruct((B, L, D), F32),
        compiler_params=_cparams("parallel", "parallel"),
        name="outffn",
    )(x, yh_t, yg, ym, mod, *consts)


def _rope_tables(L, with_pos):
    lane = np.arange(LANES)

    def pattern(dim, off, width):
        loc = (lane - off) % dim
        half = dim // 2
        active = (lane >= off) & (lane < off + width)
        use_col = loc >= half
        fi = (loc % half) % (half // 2)
        first = (loc % half) < (half // 2)
        return active, use_col, fi, first, half

    def tables(dim, off, width):
        active, use_col, fi, first, half = pattern(dim, off, width)
        if not with_pos:
            one = jnp.ones((L, LANES), F32)
            zero = jnp.zeros((L, LANES), F32)
            return [one, zero, zero]
        row = jnp.repeat(jnp.arange(L // GRID_W, dtype=jnp.int32), GRID_W).astype(F32)
        col = jnp.tile(jnp.arange(GRID_W, dtype=jnp.int32), L // GRID_W).astype(F32)
        inv = ROPE_THETA ** (-jnp.arange(0, half, 2, dtype=F32) / half)
        pos = jnp.where(jnp.asarray(use_col)[None, :], col[:, None], row[:, None])
        ang = pos * inv[jnp.asarray(fi)][None, :]
        act = jnp.asarray(active)[None, :]
        fst = jnp.asarray(first)[None, :]
        cos = jnp.where(act, jnp.cos(ang), 1.0)
        sin = jnp.sin(ang)
        s1 = jnp.where(act & fst, -sin, 0.0)
        s2 = jnp.where(act & ~fst, sin, 0.0)
        return [cos, s1, s2]

    return (tables(GQA_HEAD_DIM, 0, LANES) + tables(MLA_ROPE_DIM, MLA_NOPE_DIM, MLA_ROPE_DIM)
            + tables(MLA_ROPE_DIM, 0, MLA_ROPE_DIM))


def _gqa_tile_order():
    per = GQA_HEADS // GQA_KV_HEADS
    return [h for t in range(per) for h in (t, per + t)]


def _layer_weights(l, p):
    hd = GQA_HEAD_DIM
    order = _gqa_tile_order()
    qperm = np.concatenate([np.arange(hd) + hd * h for h in order])
    w_in = p["w_in"][l]
    w_in = jnp.concatenate([w_in[:, :IN_OFF[1]], w_in[:, IN_OFF[1]:IN_OFF[2]][:, qperm], w_in[:, IN_OFF[2]:]], axis=1)
    w_in = jnp.pad(w_in, ((0, 0), (0, IN_WIDTH_PAD - w_in.shape[1]))).astype(BF16)

    wq = p["mla_w_uq"][l].reshape(-1, MLA_HEADS, MLA_QK_DIM)
    wq = jnp.pad(wq, ((0, 0), (0, 0), (0, LANES - MLA_QK_DIM))).reshape(-1, MLA_HEADS * LANES).astype(BF16)

    wkv = p["mla_w_ukv"][l].reshape(-1, MLA_HEADS, MLA_NOPE_DIM + MLA_V_DIM)
    wk = jnp.pad(wkv[:, :, :MLA_NOPE_DIM], ((0, 0), (0, 0), (0, LANES - MLA_NOPE_DIM)))
    place = np.zeros((LANES, MLA_HEADS, LANES), np.float32)
    for h in range(MLA_HEADS):
        place[np.arange(MLA_ROPE_DIM), h, MLA_NOPE_DIM + np.arange(MLA_ROPE_DIM)] = 1.0
    wkp = jnp.concatenate([wk, jnp.asarray(place)], axis=0).reshape(-1, MLA_HEADS * LANES).astype(BF16)
    wv = wkv[:, :, MLA_NOPE_DIM:]
    zeros = jnp.zeros_like(wv)
    even = (np.arange(MLA_HEADS) % 2 == 0)[None, :, None]
    wvp = jnp.concatenate([jnp.where(even, wv, zeros), jnp.where(even, zeros, wv)], axis=2)
    wvp = wvp.reshape(-1, MLA_HEADS * LANES).astype(BF16)

    w_out = p["w_out"][l]
    g0 = HY_W
    g1 = g0 + GQA_HEADS * hd
    w_out = jnp.concatenate([w_out[:g0], w_out[g0:g1][qperm], w_out[g1:]], axis=0).astype(BF16)

    e2 = np.kron(np.eye(LANES // hd, dtype=np.float32), np.ones((hd, hd), np.float32))
    two = lambda g: jnp.tile(g, LANES // hd)[None, :]
    return {
        "g1": p["norm1_g"][l][None, :], "g2": p["norm2_g"][l][None, :],
        "w_in": w_in, "e2": jnp.asarray(e2, BF16),
        "gq": two(p["gqa_q_g"][l]), "gk": two(p["gqa_k_g"][l]),
        "gmq": p["mla_q_g"][l][None, :], "gmkv": p["mla_kv_g"][l][None, :],
        "wq": wq, "wkp": wkp, "wvp": wvp, "w_out": w_out,
        "w1": p["ffn_w1"][l].astype(BF16), "w3": p["ffn_w3"][l].astype(BF16), "w2": p["ffn_w2"][l].astype(BF16),
        "conv_w": p["hy_conv_w"][l], "conv_b": p["hy_conv_b"][l], "hy_bias": p["hy_bias"][l],
    }


def kernel(x, c, ctx, c_ctx, mod_w, mod_b, norm1_g, norm2_g, w_in, hy_conv_w, hy_conv_b, hy_filt_w1, hy_filt_b1, hy_filt_w2, hy_filt_b2, hy_filt_w3, hy_filt_freq, hy_bias, gqa_q_g, gqa_k_g, mla_q_g, mla_kv_g, mla_w_uq, mla_w_ukv, w_out, ffn_w1, ffn_w3, ffn_w2, final_g):
    p = dict(norm1_g=norm1_g, norm2_g=norm2_g, w_in=w_in, hy_conv_w=hy_conv_w, hy_conv_b=hy_conv_b,
             hy_bias=hy_bias, gqa_q_g=gqa_q_g, gqa_k_g=gqa_k_g, mla_q_g=mla_q_g, mla_kv_g=mla_kv_g,
             mla_w_uq=mla_w_uq, mla_w_ukv=mla_w_ukv, w_out=w_out, ffn_w1=ffn_w1, ffn_w3=ffn_w3, ffn_w2=ffn_w2)
    B, L, D = x.shape
    Lc = ctx.shape[1]
    depth = mod_w.shape[0]

    pad_rows = (-(B + 1)) % 8
    c_all = jnp.concatenate([c, c_ctx[None, :], jnp.zeros((pad_rows, D), F32)], axis=0)
    mod = _modulation(c_all, mod_w, mod_b).reshape(depth, -1, N_MOD, D)

    tabs_lat = _rope_tables(L, True)
    tabs_ctx = _rope_tables(Lc, False)
    final_g2 = final_g[None, :]

    x_lat, x_ctx = x, ctx
    for l in range(depth):
        last = l == depth - 1
        lw = _layer_weights(l, p)
        filt = (hy_filt_w1[l], hy_filt_b1[l], hy_filt_w2[l], hy_filt_b2[l], hy_filt_w3[l], hy_filt_freq[l])
        mod_lat = mod[l, :B]
        mod_ctx = jnp.broadcast_to(mod[l, B][None], (B, N_MOD, D))

        hy_c, q_c, k_c, v_c, qm_c, km_c, vm_c = _inproj(x_ctx, mod_ctx, lw, tabs_ctx)
        hy, q, k, v, qm, km, vm = _inproj(x_lat, mod_lat, lw, tabs_lat)

        if not last:
            yh_c = _hyena(hy_c, lw, _hyena_filters(Lc, *filt))
            yg_c = _gqa_attention(q_c, k_c, v_c)
            ym_c = _mla_attention(qm_c, km_c, vm_c)
            x_ctx_next = _outffn(x_ctx, yh_c, yg_c, ym_c, mod_ctx, lw, final_g2, False)

        yh = _hyena(hy, lw, _hyena_filters(L, *filt))
        cat = lambda a_c, a_l: jnp.concatenate([a_c, a_l], axis=1)
        yg = _gqa_attention(q, cat(k_c, k), cat(v_c, v))
        ym = _mla_attention(qm, cat(km_c, km), cat(vm_c, vm))
        x_lat = _outffn(x_lat, yh, yg, ym, mod_lat, lw, final_g2, last)
        if not last:
            x_ctx = x_ctx_next
    return x_lat
```

```python
import functools
import math

import jax
import jax.numpy as jnp
import numpy as np
from jax import lax
from jax.experimental import pallas as pl
from jax.experimental.pallas import tpu as pltpu

F32 = jnp.float32
BF16 = jnp.bfloat16

N_MOD = 6
GRID_W = 64
NORM_EPS = 1e-6
ROPE_THETA = 10000.0
HY_W = 256
HY_ORDER = 2
HY_BANDS = 16
HY_TARGET = 1e-2
HY_FAST_PCT = 0.3
HY_SLOW_PCT = 1.5
GQA_HEADS = 6
GQA_KV_HEADS = 2
GQA_HEAD_DIM = 64
MLA_HEADS = 6
MLA_NOPE_DIM = 64
MLA_ROPE_DIM = 32
MLA_V_DIM = 64
MLA_QK_DIM = MLA_NOPE_DIM + MLA_ROPE_DIM
IN_SPLITS = (768, 384, 128, 128, 256, 128, 32)
IN_OFF = tuple(int(v) for v in np.cumsum((0,) + IN_SPLITS))
IN_WIDTH_PAD = 1920

LANES = 128
VMEM_LIMIT = 56 * 1024 * 1024
CONV_BLK = 256
CONV_CH = 8

LOG2E = 1.4426950408889634
HIGHEST = lax.Precision.HIGHEST


def _cparams(*sem):
    return pltpu.CompilerParams(dimension_semantics=sem, vmem_limit_bytes=VMEM_LIMIT)


def _const_spec(shape):
    zeros = (0,) * len(shape)
    return pl.BlockSpec(shape, lambda *_: zeros, pipeline_mode=pl.Buffered(1))


def _mod_kernel(c_ref, w_ref, b_ref, o_ref):
    c = c_ref[...]
    sc = c * jax.nn.sigmoid(c)
    o_ref[0] = jnp.dot(sc, w_ref[0], precision=HIGHEST, preferred_element_type=F32) + b_ref[0]


def _modulation(c_all, mod_w, mod_b):
    depth, d, n = mod_w.shape
    rows = c_all.shape[0]
    tn = 1536
    return pl.pallas_call(
        _mod_kernel,
        grid=(depth, n // tn),
        in_specs=[
            pl.BlockSpec((rows, d), lambda l, j: (0, 0)),
            pl.BlockSpec((1, d, tn), lambda l, j: (l, 0, j)),
            pl.BlockSpec((1, 1, tn), lambda l, j: (l, 0, j)),
        ],
        out_specs=pl.BlockSpec((1, rows, tn), lambda l, j: (l, 0, j)),
        out_shape=jax.ShapeDtypeStruct((depth, rows, n), F32),
        compiler_params=_cparams("parallel", "parallel"),
        name="modulation",
    )(c_all, mod_w, mod_b.reshape(depth, 1, n))


def _filter_kernel(zt_ref, w1t_ref, b1_ref, w2t_ref, b2_ref, fr_ref, w3f_ref, w3b_ref,
                   trow_ref, sel_ref, delta_ref, o_ref, h_ref):
    @pl.when(pl.program_id(0) == 0)
    def _():
        fr = fr_ref[...]
        h = jnp.sin(fr * (jnp.dot(w1t_ref[...], zt_ref[...], precision=HIGHEST,
                                  preferred_element_type=F32) + b1_ref[...]))
        h_ref[...] = jnp.sin(fr * (jnp.dot(w2t_ref[...], h, precision=HIGHEST,
                                           preferred_element_type=F32) + b2_ref[...]))

    h = h_ref[...]
    fwd = jnp.dot(w3f_ref[...], h, precision=HIGHEST, preferred_element_type=F32)
    bwd = jnp.dot(w3b_ref[...], h, precision=HIGHEST, preferred_element_type=F32)
    sel = sel_ref[...]
    k = jnp.where(sel > 0.0, fwd, jnp.where(sel < 0.0, bwd, 0.0))
    k = k * jnp.exp(-delta_ref[...] * trow_ref[...])
    nrm = jnp.sum(jnp.abs(k), axis=1, keepdims=True)
    o_ref[0] = k / nrm


def _filter_features(L):
    t = np.linspace(0.0, 1.0, L, dtype=np.float32)
    wpos = np.float32(2.0 * math.pi) * np.arange(L, dtype=np.float32) / np.float32(L)
    f = np.linspace(1e-4, HY_BANDS - 1, HY_BANDS, dtype=np.float32)
    ang = (f[None, :] * wpos[:, None]).astype(np.float64)
    z = np.concatenate([t[:, None], np.cos(ang), -np.sin(ang)], axis=1).astype(np.float32)
    n = np.arange(2 * L)
    lag = np.where(n <= L, np.minimum(n, L - 1), 2 * L - n)
    sel = np.where(n < L, 1.0, np.where(n > L, -1.0, 0.0)).astype(np.float32)
    zt = np.zeros((LANES, 2 * L), np.float32)
    zt[:z.shape[1]] = z[lag].T
    return zt, t[lag][None, :], sel[None, :]


def _hyena_filters(L, w1, b1, w2, b2, w3, freq):
    P = 2 * L
    zt, trow, sel = _filter_features(L)
    nf = w1.shape[1]
    w1t = jnp.pad(w1, ((0, LANES - w1.shape[0]), (0, 0))).T
    col = lambda a: a[:, None]
    deltas = np.abs(np.linspace(math.log(HY_TARGET) / HY_FAST_PCT, math.log(HY_TARGET) / HY_SLOW_PCT,
                                HY_W, dtype=np.float32))[:, None]
    halves = HY_W // LANES
    cb = 2 * halves
    whole = lambda shape: pl.BlockSpec(shape, lambda s: (0, 0))
    return pl.pallas_call(
        _filter_kernel,
        grid=(HY_ORDER * halves,),
        in_specs=[
            whole((LANES, P)), whole((nf, LANES)), whole((nf, 1)), whole((nf, nf)), whole((nf, 1)),
            whole((nf, 1)),
            pl.BlockSpec((LANES, nf), lambda s: ((s // halves) * cb + s % halves, 0)),
            pl.BlockSpec((LANES, nf), lambda s: ((s // halves) * cb + halves + s % halves, 0)),
            whole((1, P)), whole((1, P)),
            pl.BlockSpec((LANES, 1), lambda s: (s % halves, 0)),
        ],
        out_specs=pl.BlockSpec((1, LANES, P), lambda s: (s // halves, s % halves, 0)),
        out_shape=jax.ShapeDtypeStruct((HY_ORDER, HY_W, P), F32),
        scratch_shapes=[pltpu.VMEM((nf, P), F32)],
        compiler_params=_cparams("arbitrary"),
        name="hyena_filters",
    )(zt, w1t, col(b1), w2.T, col(b2), col(freq), w3.T, w3.T, trow, sel, deltas)


def _rope(t, c_ref, s1_ref, s2_ref, sh):
    return (t * c_ref[...] + pltpu.roll(t, LANES - sh, 1) * s1_ref[...]
            + pltpu.roll(t, sh, 1) * s2_ref[...])


def _inproj_kernel(x_ref, mod_ref, g1_ref, w_ref, e2_ref, gq_ref, gk_ref, gmq_ref, gmkv_ref,
                   wq_ref, wkp_ref, wvp_ref,
                   cg_ref, s1g_ref, s2g_ref, cm_ref, s1m_ref, s2m_ref, ck_ref, s1k_ref, s2k_ref,
                   hy_ref, q_ref, k_ref, v_ref, qm_ref, km_ref, vm_ref, *, q_scale, qm_scale):
    x = x_ref[0]
    shift = mod_ref[0, 0:1, :]
    scale = mod_ref[0, 1:2, :]
    h = x * lax.rsqrt(jnp.mean(x * x, axis=-1, keepdims=True) + NORM_EPS) * g1_ref[...]
    h = h * (1.0 + scale) + shift
    u = jnp.dot(h.astype(BF16), w_ref[...], preferred_element_type=F32)

    hy_ref[0] = u[:, IN_OFF[0]:IN_OFF[1]]

    e2 = e2_ref[...]

    def head_norm(t, g):
        sq = t * t
        hi = sq.astype(BF16)
        lo = (sq - hi.astype(F32)).astype(BF16)
        ss = (jnp.dot(hi, e2, preferred_element_type=F32) + jnp.dot(lo, e2, preferred_element_type=F32))
        return t * lax.rsqrt(ss * (1.0 / GQA_HEAD_DIM) + NORM_EPS) * g

    for i in range(GQA_HEADS * GQA_HEAD_DIM // LANES):
        t = u[:, IN_OFF[1] + LANES * i: IN_OFF[1] + LANES * (i + 1)]
        t = _rope(head_norm(t, gq_ref[...]), cg_ref, s1g_ref, s2g_ref, GQA_HEAD_DIM // 4)
        q_ref[0, :, LANES * i: LANES * (i + 1)] = (t * q_scale).astype(BF16)
    t = u[:, IN_OFF[2]:IN_OFF[3]]
    k_ref[0] = _rope(head_norm(t, gk_ref[...]), cg_ref, s1g_ref, s2g_ref, GQA_HEAD_DIM // 4).astype(BF16)
    v_ref[0] = u[:, IN_OFF[3]:IN_OFF[4]].astype(BF16)

    mq = u[:, IN_OFF[4]:IN_OFF[5]]
    mqn = mq * lax.rsqrt(jnp.mean(mq * mq, axis=-1, keepdims=True) + NORM_EPS) * gmq_ref[...]
    qm = jnp.dot(mqn.astype(BF16), wq_ref[...], preferred_element_type=F32)
    for hd in range(MLA_HEADS):
        t = _rope(qm[:, LANES * hd: LANES * (hd + 1)], cm_ref, s1m_ref, s2m_ref, MLA_ROPE_DIM // 4)
        qm_ref[0, :, LANES * hd: LANES * (hd + 1)] = (t * qm_scale).astype(BF16)

    mkv = u[:, IN_OFF[5]:IN_OFF[6]]
    kvn = (mkv * lax.rsqrt(jnp.mean(mkv * mkv, axis=-1, keepdims=True) + NORM_EPS) * gmkv_ref[...]).astype(BF16)
    kpe = _rope(u[:, IN_OFF[6]:IN_OFF[6] + LANES], ck_ref, s1k_ref, s2k_ref, MLA_ROPE_DIM // 4).astype(BF16)
    kcat = jnp.concatenate([kvn, kpe], axis=1)
    km_ref[0] = jnp.dot(kcat, wkp_ref[...], preferred_element_type=F32).astype(BF16)
    vm_ref[0] = jnp.dot(kvn, wvp_ref[...], preferred_element_type=F32).astype(BF16)


def _inproj(x, mod, lw, tabs):
    B, L, D = x.shape
    tm = min(512, L)
    nh = MLA_HEADS * LANES
    tab_spec = pl.BlockSpec((tm, LANES), lambda i, b: (i, 0))
    tok = lambda w: pl.BlockSpec((1, tm, w), lambda i, b: (b, i, 0))
    kern = functools.partial(_inproj_kernel,
                             q_scale=GQA_HEAD_DIM ** -0.5 * LOG2E, qm_scale=MLA_QK_DIM ** -0.5 * LOG2E)
    consts = [lw["g1"], lw["w_in"], lw["e2"], lw["gq"], lw["gk"], lw["gmq"], lw["gmkv"],
              lw["wq"], lw["wkp"], lw["wvp"]]
    out_w = [(IN_SPLITS[0], F32), (IN_SPLITS[1], BF16), (IN_SPLITS[2], BF16), (IN_SPLITS[3], BF16),
             (nh, BF16), (nh, BF16), (nh, BF16)]
    return pl.pallas_call(
        kern,
        grid=(L // tm, B),
        in_specs=[tok(D), pl.BlockSpec((1, N_MOD, D), lambda i, b: (b, 0, 0))]
        + [_const_spec(a.shape) for a in consts] + [tab_spec] * 9,
        out_specs=[tok(w) for w, _ in out_w],
        out_shape=[jax.ShapeDtypeStruct((B, L, w), dt) for w, dt in out_w],
        compiler_params=_cparams("parallel", "parallel"),
        name="inproj",
    )(x, mod, *consts, *tabs)


def _shortconv_kernel(u_ref, w_ref, b_ref, o_ref):
    u = u_ref[0]
    L = u.shape[0]
    row = lax.broadcasted_iota(jnp.int32, u.shape, 0)
    prev = jnp.where(row == 0, 0.0, pltpu.roll(u, 1, 0))
    nxt = jnp.where(row == L - 1, 0.0, pltpu.roll(u, L - 1, 0))
    y = prev * w_ref[0, 0:1, :] + u * w_ref[0, 1:2, :] + nxt * w_ref[0, 2:3, :] + b_ref[0]
    o_ref[0, 0] = y.T


def _shortconv_t(hy, conv_w, conv_b):
    B, L, _ = hy.shape
    w = jnp.transpose(conv_w.reshape(3, 3, HY_W), (1, 0, 2))
    b = conv_b.reshape(3, 1, HY_W)
    return pl.pallas_call(
        _shortconv_kernel,
        grid=(B, 3),
        in_specs=[
            pl.BlockSpec((1, L, HY_W), lambda bb, p: (bb, 0, p)),
            pl.BlockSpec((1, 3, HY_W), lambda bb, p: (p, 0, 0)),
            pl.BlockSpec((1, 1, HY_W), lambda bb, p: (p, 0, 0)),
        ],
        out_specs=pl.BlockSpec((1, 1, HY_W, L), lambda bb, p: (p, bb, 0, 0)),
        out_shape=jax.ShapeDtypeStruct((3, B, HY_W, L), F32),
        compiler_params=_cparams("parallel", "parallel"),
        name="hyena_shortconv",
    )(hy, w, b)


def _longconv_kernel(bias_ref, kf_ref, u_ref, o_ref, s0_ref, s1_ref, acc_ref, x_ref, y_ref, *, nblk):
    g = pl.program_id(0)
    B, CH = u_ref.shape[1], u_ref.shape[2]
    P = s0_ref.shape[1]
    for p in range(3):
        x_ref[p] = jnp.swapaxes(u_ref[p], 0, 1)

    def to_rows(z):
        return jnp.concatenate([z[:, j * CONV_BLK:(j + 1) * CONV_BLK] for j in range(nblk)], axis=0)

    def conv(zin, o, c, s_ref):
        krow = kf_ref[o, pl.ds(c, 1), :]
        s_ref[...] = pltpu.roll(jnp.broadcast_to(krow, (CONV_BLK, P)), 0, 1,
                                stride=1, stride_axis=0).astype(BF16)
        zb = zin.astype(BF16)
        acc_ref[...] = jnp.zeros_like(acc_ref)
        for d in range(-(nblk - 1), nblk):
            j0 = max(0, -d)
            n = nblk - abs(d)
            i0 = j0 + d
            col = (d * CONV_BLK) % P
            acc_ref[B * i0: B * (i0 + n), :] += jnp.dot(
                zb[B * j0: B * (j0 + n), :], s_ref[:, col: col + CONV_BLK],
                preferred_element_type=F32)
        return acc_ref[...] + zin * bias_ref[o, g * CH + c]

    def channel(c, carry):
        z1 = to_rows(x_ref[1, c]) * conv(to_rows(x_ref[0, c]), 0, c, s0_ref)
        y = to_rows(x_ref[2, c]) * conv(z1, 1, c, s1_ref)
        y_ref[c] = jnp.concatenate([y[B * i: B * (i + 1)] for i in range(nblk)], axis=1)
        return carry

    lax.fori_loop(0, CH, channel, 0)
    o_ref[...] = jnp.swapaxes(y_ref[...], 0, 1)


def _longconv(ut, kf, hy_bias):
    _, B, C, L = ut.shape
    P = kf.shape[-1]
    nblk = L // CONV_BLK
    return pl.pallas_call(
        functools.partial(_longconv_kernel, nblk=nblk),
        grid=(C // CONV_CH,),
        in_specs=[
            pl.BlockSpec(memory_space=pltpu.SMEM),
            pl.BlockSpec((HY_ORDER, CONV_CH, P), lambda g: (0, g, 0)),
            pl.BlockSpec((3, B, CONV_CH, L), lambda g: (0, 0, g, 0)),
        ],
        out_specs=pl.BlockSpec((B, CONV_CH, L), lambda g: (0, g, 0)),
        out_shape=jax.ShapeDtypeStruct((B, C, L), F32),
        scratch_shapes=[pltpu.VMEM((CONV_BLK, P), BF16), pltpu.VMEM((CONV_BLK, P), BF16),
                        pltpu.VMEM((nblk * B, CONV_BLK), F32),
                        pltpu.VMEM((3, CONV_CH, B, L), F32), pltpu.VMEM((CONV_CH, B, L), F32)],
        compiler_params=_cparams("parallel"),
        name="hyena_longconv",
    )(hy_bias, kf, ut)


def _hyena(hy, lw, kf):
    ut = _shortconv_t(hy, lw["conv_w"], lw["conv_b"])
    return _longconv(ut, kf, lw["hy_bias"])


def _attn_units_step(qs, k_refs, v_refs, s_ref, m_ref, koffs, tk):
    nu = len(qs)
    M = qs[0].shape[0]
    m_old = [m_ref[u] for u in range(nu)]
    m_new = [jnp.full((M, LANES), -jnp.inf, F32) for _ in range(nu)]
    acc = [jnp.zeros((M, 2 * LANES), F32) for _ in range(nu)]
    ones = jnp.ones((tk, LANES), BF16)
    col = 0
    for k_ref, v_ref in zip(k_refs, v_refs):
        for c in range(k_ref.shape[1] // tk):
            rows = slice(c * tk, (c + 1) * tk)
            cols = slice(col, col + tk)
            col += tk
            for u in range(nu):
                lanes = slice(koffs[u], koffs[u] + LANES)
                s_old = s_ref[u, :, cols]
                s = lax.dot_general(qs[u], k_ref[0, rows, lanes], (((1,), (1,)), ((), ())),
                                    preferred_element_type=F32)
                s_ref[u, :, cols] = s
                for j in range(tk // LANES):
                    m_new[u] = jnp.maximum(m_new[u], s[:, j * LANES:(j + 1) * LANES])
                p = jnp.concatenate([jnp.exp2(s_old[:, j * LANES:(j + 1) * LANES] - m_old[u])
                                     for j in range(tk // LANES)], axis=1).astype(BF16)
                vc = jnp.concatenate([v_ref[0, rows, lanes], ones], axis=1)
                acc[u] = acc[u] + jnp.dot(p, vc, preferred_element_type=F32)
    outs = []
    for u in range(nu):
        m_ref[u] = jnp.broadcast_to(jnp.max(m_new[u], axis=-1, keepdims=True), (M, LANES))
        outs.append(acc[u][:, :LANES] * (1.0 / acc[u][:, LANES:]))
    return outs


def _attn_init(s_ref, m_ref):
    @pl.when(pl.program_id(0) == 0)
    def _():
        s_ref[...] = jnp.zeros_like(s_ref)
        m_ref[...] = jnp.zeros_like(m_ref)


def _gqa_kernel(*refs, nseg, tk):
    q_ref, k_refs, v_refs = refs[0], refs[1:1 + nseg], refs[1 + nseg:1 + 2 * nseg]
    o_ref, s_ref, m_ref = refs[1 + 2 * nseg:]
    _attn_init(s_ref, m_ref)
    tq = q_ref.shape[1]
    ntile = q_ref.shape[2] // LANES
    low = lax.broadcasted_iota(jnp.int32, (tq, LANES), 1) < GQA_HEAD_DIM
    zero = jnp.zeros((tq, LANES), BF16)
    qs = []
    for t in range(ntile):
        qt = q_ref[0, :, LANES * t: LANES * (t + 1)]
        qs.append(jnp.where(low, qt, zero))
        qs.append(jnp.where(low, zero, qt))
    (o,) = _attn_units_step([jnp.concatenate(qs, axis=0)], k_refs, v_refs, s_ref, m_ref, (0,), tk)
    for t in range(ntile):
        oa = o[(2 * t) * tq:(2 * t + 1) * tq]
        ob = o[(2 * t + 1) * tq:(2 * t + 2) * tq]
        o_ref[0, :, LANES * t: LANES * (t + 1)] = jnp.where(low, oa, ob).astype(BF16)


def _gqa_attention(q, ks, vs):
    B, L, W = q.shape
    T = sum(k.shape[1] for k in ks)
    tq = min(128, L)
    nq = L // tq
    last = B * nq - 1
    new = lambda g: jnp.minimum(g, last)
    old = lambda g: jnp.maximum(g - 1, 0)
    return pl.pallas_call(
        functools.partial(_gqa_kernel, nseg=len(ks), tk=256),
        grid=(B * nq + 1,),
        in_specs=[pl.BlockSpec((1, tq, W), lambda g: (new(g) // nq, new(g) % nq, 0))]
        + [pl.BlockSpec((1, k.shape[1], LANES), lambda g: (new(g) // nq, 0, 0)) for k in ks]
        + [pl.BlockSpec((1, v.shape[1], LANES), lambda g: (old(g) // nq, 0, 0)) for v in vs],
        out_specs=pl.BlockSpec((1, tq, W), lambda g: (old(g) // nq, old(g) % nq, 0)),
        out_shape=jax.ShapeDtypeStruct((B, L, W), BF16),
        scratch_shapes=[pltpu.VMEM((1, GQA_HEADS * tq, T), F32), pltpu.VMEM((1, GQA_HEADS * tq, LANES), F32)],
        compiler_params=_cparams("arbitrary"),
        name="gqa_attention",
    )(q, *ks, *vs)


def _mla_kernel(*refs, nseg, tk):
    q_ref, k_refs, v_refs = refs[0], refs[1:1 + nseg], refs[1 + nseg:1 + 2 * nseg]
    o_ref, s_ref, m_ref = refs[1 + 2 * nseg:]
    _attn_init(s_ref, m_ref)
    qs = [q_ref[0, :, 0:LANES], q_ref[0, :, LANES:2 * LANES]]
    oa, ob = _attn_units_step(qs, k_refs, v_refs, s_ref, m_ref, (0, LANES), tk)
    o_ref[0] = (oa + ob).astype(BF16)


def _mla_attention(q, ks, vs):
    B, L, W = q.shape
    T = sum(k.shape[1] for k in ks)
    npair = W // (2 * LANES)
    tq = min(256, L)
    nq = L // tq
    last = B * npair * nq - 1
    new = lambda g: jnp.minimum(g, last)
    old = lambda g: jnp.maximum(g - 1, 0)
    bidx = lambda u: u // (npair * nq)
    pidx = lambda u: (u // nq) % npair
    return pl.pallas_call(
        functools.partial(_mla_kernel, nseg=len(ks), tk=256),
        grid=(B * npair * nq + 1,),
        in_specs=[pl.BlockSpec((1, tq, 2 * LANES), lambda g: (bidx(new(g)), new(g) % nq, pidx(new(g))))]
        + [pl.BlockSpec((1, k.shape[1], 2 * LANES), lambda g: (bidx(new(g)), 0, pidx(new(g)))) for k in ks]
        + [pl.BlockSpec((1, v.shape[1], 2 * LANES), lambda g: (bidx(old(g)), 0, pidx(old(g)))) for v in vs],
        out_specs=pl.BlockSpec((1, tq, LANES), lambda g: (bidx(old(g)), old(g) % nq, pidx(old(g)))),
        out_shape=jax.ShapeDtypeStruct((B, L, npair * LANES), BF16),
        scratch_shapes=[pltpu.VMEM((2, tq, T), F32), pltpu.VMEM((2, tq, LANES), F32)],
        compiler_params=_cparams("arbitrary"),
        name="mla_attention",
    )(q, *ks, *vs)


def _outffn_kernel(x_ref, yh_ref, yg_ref, ym_ref, mod_ref, g2_ref, wo_ref, w1_ref, w3_ref, w2_ref,
                   gf_ref, o_ref, *, hchunk, final):
    x = x_ref[0]
    mix = jnp.concatenate([yh_ref[0].T.astype(BF16), yg_ref[0], ym_ref[0]], axis=1)
    y = jnp.dot(mix, wo_ref[...], preferred_element_type=F32)
    x1 = x + mod_ref[0, 2:3, :] * y
    h = x1 * lax.rsqrt(jnp.mean(x1 * x1, axis=-1, keepdims=True) + NORM_EPS) * g2_ref[...]
    h = (h * (1.0 + mod_ref[0, 4:5, :]) + mod_ref[0, 3:4, :]).astype(BF16)
    f = jnp.zeros_like(x)
    hidden = w1_ref.shape[1]
    for c in range(hidden // hchunk):
        a = jnp.dot(h, w1_ref[:, c * hchunk:(c + 1) * hchunk], preferred_element_type=F32)
        b = jnp.dot(h, w3_ref[:, c * hchunk:(c + 1) * hchunk], preferred_element_type=F32)
        g = (a * jax.nn.sigmoid(a) * b).astype(BF16)
        f = f + jnp.dot(g, w2_ref[c * hchunk:(c + 1) * hchunk, :], preferred_element_type=F32)
    x2 = x1 + mod_ref[0, 5:6, :] * f
    if final:
        x2 = x2 * lax.rsqrt(jnp.mean(x2 * x2, axis=-1, keepdims=True) + NORM_EPS) * gf_ref[...]
    o_ref[0] = x2


def _outffn(x, yh_t, yg, ym, mod, lw, final_g, final):
    B, L, D = x.shape
    tm = min(512, L)
    hidden = lw["w1"].shape[1]
    consts = [lw["g2"], lw["w_out"], lw["w1"], lw["w3"], lw["w2"], final_g]
    kern = functools.partial(_outffn_kernel, hchunk=hidden // 2, final=final)
    tok = lambda w: pl.BlockSpec((1, tm, w), lambda b, i: (b, i, 0))
    return pl.pallas_call(
        kern,
        grid=(B, L // tm),
        in_specs=[tok(D), pl.BlockSpec((1, HY_W, tm), lambda b, i: (b, 0, i)),
                  tok(yg.shape[2]), tok(ym.shape[2]),
                  pl.BlockSpec((1, N_MOD, D), lambda b, i: (b, 0, 0))]
        + [_const_spec(a.shape) for a in consts],
        out_specs=tok(D),
        out_shape=jax.ShapeDtypeStruct((B, L, D), F32),
        compiler_params=_cparams("parallel", "parallel"),
        name="outffn",
    )(x, yh_t, yg, ym, mod, *consts)


def _rope_tables(L, with_pos):
    lane = np.arange(LANES)

    def pattern(dim, off, width):
        loc = (lane - off) % dim
        half = dim // 2
        active = (lane >= off) & (lane < off + width)
        use_col = loc >= half
        fi = (loc % half) % (half // 2)
        first = (loc % half) < (half // 2)
        return active, use_col, fi, first, half

    def tables(dim, off, width):
        active, use_col, fi, first, half = pattern(dim, off, width)
        if not with_pos:
            one = jnp.ones((L, LANES), F32)
            zero = jnp.zeros((L, LANES), F32)
            return [one, zero, zero]
        row = jnp.repeat(jnp.arange(L // GRID_W, dtype=jnp.int32), GRID_W).astype(F32)
        col = jnp.tile(jnp.arange(GRID_W, dtype=jnp.int32), L // GRID_W).astype(F32)
        inv = ROPE_THETA ** (-jnp.arange(0, half, 2, dtype=F32) / half)
        pos = jnp.where(jnp.asarray(use_col)[None, :], col[:, None], row[:, None])
        ang = pos * inv[jnp.asarray(fi)][None, :]
        act = jnp.asarray(active)[None, :]
        fst = jnp.asarray(first)[None, :]
        cos = jnp.where(act, jnp.cos(ang), 1.0)
        sin = jnp.sin(ang)
        s1 = jnp.where(act & fst, -sin, 0.0)
        s2 = jnp.where(act & ~fst, sin, 0.0)
        return [cos, s1, s2]

    return (tables(GQA_HEAD_DIM, 0, LANES) + tables(MLA_ROPE_DIM, MLA_NOPE_DIM, MLA_ROPE_DIM)
            + tables(MLA_ROPE_DIM, 0, MLA_ROPE_DIM))


def _gqa_tile_order():
    per = GQA_HEADS // GQA_KV_HEADS
    return [h for t in range(per) for h in (t, per + t)]


def _layer_weights(l, p):
    hd = GQA_HEAD_DIM
    order = _gqa_tile_order()
    qperm = np.concatenate([np.arange(hd) + hd * h for h in order])
    w_in = p["w_in"][l]
    w_in = jnp.concatenate([w_in[:, :IN_OFF[1]], w_in[:, IN_OFF[1]:IN_OFF[2]][:, qperm], w_in[:, IN_OFF[2]:]], axis=1)
    w_in = jnp.pad(w_in, ((0, 0), (0, IN_WIDTH_PAD - w_in.shape[1]))).astype(BF16)

    wq = p["mla_w_uq"][l].reshape(-1, MLA_HEADS, MLA_QK_DIM)
    wq = jnp.pad(wq, ((0, 0), (0, 0), (0, LANES - MLA_QK_DIM))).reshape(-1, MLA_HEADS * LANES).astype(BF16)

    wkv = p["mla_w_ukv"][l].reshape(-1, MLA_HEADS, MLA_NOPE_DIM + MLA_V_DIM)
    wk = jnp.pad(wkv[:, :, :MLA_NOPE_DIM], ((0, 0), (0, 0), (0, LANES - MLA_NOPE_DIM)))
    place = np.zeros((LANES, MLA_HEADS, LANES), np.float32)
    for h in range(MLA_HEADS):
        place[np.arange(MLA_ROPE_DIM), h, MLA_NOPE_DIM + np.arange(MLA_ROPE_DIM)] = 1.0
    wkp = jnp.concatenate([wk, jnp.asarray(place)], axis=0).reshape(-1, MLA_HEADS * LANES).astype(BF16)
    wv = wkv[:, :, MLA_NOPE_DIM:]
    zeros = jnp.zeros_like(wv)
    even = (np.arange(MLA_HEADS) % 2 == 0)[None, :, None]
    wvp = jnp.concatenate([jnp.where(even, wv, zeros), jnp.where(even, zeros, wv)], axis=2)
    wvp = wvp.reshape(-1, MLA_HEADS * LANES).astype(BF16)

    w_out = p["w_out"][l]
    g0 = HY_W
    g1 = g0 + GQA_HEADS * hd
    w_out = jnp.concatenate([w_out[:g0], w_out[g0:g1][qperm], w_out[g1:]], axis=0).astype(BF16)

    e2 = np.kron(np.eye(LANES // hd, dtype=np.float32), np.ones((hd, hd), np.float32))
    two = lambda g: jnp.tile(g, LANES // hd)[None, :]
    return {
        "g1": p["norm1_g"][l][None, :], "g2": p["norm2_g"][l][None, :],
        "w_in": w_in, "e2": jnp.asarray(e2, BF16),
        "gq": two(p["gqa_q_g"][l]), "gk": two(p["gqa_k_g"][l]),
        "gmq": p["mla_q_g"][l][None, :], "gmkv": p["mla_kv_g"][l][None, :],
        "wq": wq, "wkp": wkp, "wvp": wvp, "w_out": w_out,
        "w1": p["ffn_w1"][l].astype(BF16), "w3": p["ffn_w3"][l].astype(BF16), "w2": p["ffn_w2"][l].astype(BF16),
        "conv_w": p["hy_conv_w"][l], "conv_b": p["hy_conv_b"][l], "hy_bias": p["hy_bias"][l],
    }


def kernel(x, c, ctx, c_ctx, mod_w, mod_b, norm1_g, norm2_g, w_in, hy_conv_w, hy_conv_b, hy_filt_w1, hy_filt_b1, hy_filt_w2, hy_filt_b2, hy_filt_w3, hy_filt_freq, hy_bias, gqa_q_g, gqa_k_g, mla_q_g, mla_kv_g, mla_w_uq, mla_w_ukv, w_out, ffn_w1, ffn_w3, ffn_w2, final_g):
    p = dict(norm1_g=norm1_g, norm2_g=norm2_g, w_in=w_in, hy_conv_w=hy_conv_w, hy_conv_b=hy_conv_b,
             hy_bias=hy_bias, gqa_q_g=gqa_q_g, gqa_k_g=gqa_k_g, mla_q_g=mla_q_g, mla_kv_g=mla_kv_g,
             mla_w_uq=mla_w_uq, mla_w_ukv=mla_w_ukv, w_out=w_out, ffn_w1=ffn_w1, ffn_w3=ffn_w3, ffn_w2=ffn_w2)
    B, L, D = x.shape
    Lc = ctx.shape[1]
    depth = mod_w.shape[0]

    pad_rows = (-(B + 1)) % 8
    c_all = jnp.concatenate([c, c_ctx[None, :], jnp.zeros((pad_rows, D), F32)], axis=0)
    mod = _modulation(c_all, mod_w, mod_b).reshape(depth, -1, N_MOD, D)

    tabs_lat = _rope_tables(L, True)
    tabs_ctx = _rope_tables(Lc, False)
    final_g2 = final_g[None, :]

    x_lat, x_ctx = x, ctx
    for l in range(depth):
        last = l == depth - 1
        lw = _layer_weights(l, p)
        filt = (hy_filt_w1[l], hy_filt_b1[l], hy_filt_w2[l], hy_filt_b2[l], hy_filt_w3[l], hy_filt_freq[l])
        mod_lat = mod[l, :B]
        mod_ctx = jnp.broadcast_to(mod[l, B][None], (B, N_MOD, D))

        hy_c, q_c, k_c, v_c, qm_c, km_c, vm_c = _inproj(x_ctx, mod_ctx, lw, tabs_ctx)
        hy, q, k, v, qm, km, vm = _inproj(x_lat, mod_lat, lw, tabs_lat)

        if not last:
            yh_c = _hyena(hy_c, lw, _hyena_filters(Lc, *filt))
            yg_c = _gqa_attention(q_c, [k_c], [v_c])
            ym_c = _mla_attention(qm_c, [km_c], [vm_c])
            x_ctx_next = _outffn(x_ctx, yh_c, yg_c, ym_c, mod_ctx, lw, final_g2, False)

        yh = _hyena(hy, lw, _hyena_filters(L, *filt))
        yg = _gqa_attention(q, [k, k_c], [v, v_c])
        ym = _mla_attention(qm, [km, km_c], [vm, vm_c])
        x_lat = _outffn(x_lat, yh, yg, ym, mod_lat, lw, final_g2, last)
        if not last:
            x_ctx = x_ctx_next
    return x_lat
```

```python
import functools
import math

import jax
import jax.numpy as jnp
import numpy as np
from jax import lax
from jax.experimental import pallas as pl
from jax.experimental.pallas import tpu as pltpu

F32 = jnp.float32
BF16 = jnp.bfloat16

N_MOD = 6
GRID_W = 64
NORM_EPS = 1e-6
ROPE_THETA = 10000.0
HY_W = 256
HY_ORDER = 2
HY_BANDS = 16
HY_TARGET = 1e-2
HY_FAST_PCT = 0.3
HY_SLOW_PCT = 1.5
GQA_HEADS = 6
GQA_KV_HEADS = 2
GQA_HEAD_DIM = 64
MLA_HEADS = 6
MLA_NOPE_DIM = 64
MLA_ROPE_DIM = 32
MLA_V_DIM = 64
MLA_QK_DIM = MLA_NOPE_DIM + MLA_ROPE_DIM
IN_SPLITS = (768, 384, 128, 128, 256, 128, 32)
IN_OFF = tuple(int(v) for v in np.cumsum((0,) + IN_SPLITS))
IN_WIDTH_PAD = 1920

LANES = 128
MXU_TILE = 256
VMEM_LIMIT = 56 * 1024 * 1024
CONV_BLK = 256
CONV_CH = 8

LOG2E = 1.4426950408889634
HIGHEST = lax.Precision.HIGHEST


def _cparams(*sem):
    return pltpu.CompilerParams(dimension_semantics=sem, vmem_limit_bytes=VMEM_LIMIT)


def _const_spec(shape):
    zeros = (0,) * len(shape)
    return pl.BlockSpec(shape, lambda *_: zeros, pipeline_mode=pl.Buffered(1))


def _mod_kernel(c_ref, w_ref, b_ref, o_ref):
    c = c_ref[...]
    sc = c * jax.nn.sigmoid(c)
    o_ref[0] = jnp.dot(sc, w_ref[0], precision=HIGHEST, preferred_element_type=F32) + b_ref[0]


def _modulation(c_all, mod_w, mod_b):
    depth, d, n = mod_w.shape
    rows = c_all.shape[0]
    tn = 1536
    return pl.pallas_call(
        _mod_kernel,
        grid=(depth, n // tn),
        in_specs=[
            pl.BlockSpec((rows, d), lambda l, j: (0, 0)),
            pl.BlockSpec((1, d, tn), lambda l, j: (l, 0, j)),
            pl.BlockSpec((1, 1, tn), lambda l, j: (l, 0, j)),
        ],
        out_specs=pl.BlockSpec((1, rows, tn), lambda l, j: (l, 0, j)),
        out_shape=jax.ShapeDtypeStruct((depth, rows, n), F32),
        compiler_params=_cparams("parallel", "parallel"),
        name="modulation",
    )(c_all, mod_w, mod_b.reshape(depth, 1, n))


def _filter_kernel(zt_ref, w1t_ref, b1_ref, w2t_ref, b2_ref, fr_ref, w3f_ref, w3b_ref,
                   trow_ref, sel_ref, delta_ref, o_ref, h_ref):
    @pl.when(pl.program_id(0) == 0)
    def _():
        fr = fr_ref[...]
        h = jnp.sin(fr * (jnp.dot(w1t_ref[...], zt_ref[...], precision=HIGHEST,
                                  preferred_element_type=F32) + b1_ref[...]))
        h_ref[...] = jnp.sin(fr * (jnp.dot(w2t_ref[...], h, precision=HIGHEST,
                                           preferred_element_type=F32) + b2_ref[...]))

    h = h_ref[...]
    fwd = jnp.dot(w3f_ref[...], h, precision=HIGHEST, preferred_element_type=F32)
    bwd = jnp.dot(w3b_ref[...], h, precision=HIGHEST, preferred_element_type=F32)
    sel = sel_ref[...]
    k = jnp.where(sel > 0.0, fwd, jnp.where(sel < 0.0, bwd, 0.0))
    k = k * jnp.exp(-delta_ref[...] * trow_ref[...])
    nrm = jnp.sum(jnp.abs(k), axis=1, keepdims=True)
    o_ref[0] = k / nrm


def _filter_features(L):
    t = np.linspace(0.0, 1.0, L, dtype=np.float32)
    wpos = np.float32(2.0 * math.pi) * np.arange(L, dtype=np.float32) / np.float32(L)
    f = np.linspace(1e-4, HY_BANDS - 1, HY_BANDS, dtype=np.float32)
    ang = (f[None, :] * wpos[:, None]).astype(np.float64)
    z = np.concatenate([t[:, None], np.cos(ang), -np.sin(ang)], axis=1).astype(np.float32)
    n = np.arange(2 * L)
    lag = np.where(n <= L, np.minimum(n, L - 1), 2 * L - n)
    sel = np.where(n < L, 1.0, np.where(n > L, -1.0, 0.0)).astype(np.float32)
    zt = np.zeros((LANES, 2 * L), np.float32)
    zt[:z.shape[1]] = z[lag].T
    return zt, t[lag][None, :], sel[None, :]


def _hyena_filters(L, w1, b1, w2, b2, w3, freq):
    P = 2 * L
    zt, trow, sel = _filter_features(L)
    nf = w1.shape[1]
    w1t = jnp.pad(w1, ((0, LANES - w1.shape[0]), (0, 0))).T
    col = lambda a: a[:, None]
    deltas = np.abs(np.linspace(math.log(HY_TARGET) / HY_FAST_PCT, math.log(HY_TARGET) / HY_SLOW_PCT,
                                HY_W, dtype=np.float32))[:, None]
    halves = HY_W // LANES
    cb = 2 * halves
    whole = lambda shape: pl.BlockSpec(shape, lambda s: (0, 0))
    return pl.pallas_call(
        _filter_kernel,
        grid=(HY_ORDER * halves,),
        in_specs=[
            whole((LANES, P)), whole((nf, LANES)), whole((nf, 1)), whole((nf, nf)), whole((nf, 1)),
            whole((nf, 1)),
            pl.BlockSpec((LANES, nf), lambda s: ((s // halves) * cb + s % halves, 0)),
            pl.BlockSpec((LANES, nf), lambda s: ((s // halves) * cb + halves + s % halves, 0)),
            whole((1, P)), whole((1, P)),
            pl.BlockSpec((LANES, 1), lambda s: (s % halves, 0)),
        ],
        out_specs=pl.BlockSpec((1, LANES, P), lambda s: (s // halves, s % halves, 0)),
        out_shape=jax.ShapeDtypeStruct((HY_ORDER, HY_W, P), F32),
        scratch_shapes=[pltpu.VMEM((nf, P), F32)],
        compiler_params=_cparams("arbitrary"),
        name="hyena_filters",
    )(zt, w1t, col(b1), w2.T, col(b2), col(freq), w3.T, w3.T, trow, sel, deltas)


def _rope(t, c_ref, s1_ref, s2_ref, sh):
    return (t * c_ref[...] + pltpu.roll(t, LANES - sh, 1) * s1_ref[...]
            + pltpu.roll(t, sh, 1) * s2_ref[...])


def _inproj_kernel(x_ref, mod_ref, g1_ref, w_ref, e2_ref, gq_ref, gk_ref, gmq_ref, gmkv_ref,
                   wq_ref, wkp_ref, wvp_ref,
                   cg_ref, s1g_ref, s2g_ref, cm_ref, s1m_ref, s2m_ref, ck_ref, s1k_ref, s2k_ref,
                   hy_ref, q_ref, k_ref, vt_ref, qm_ref, km_ref, vmt_ref, *, q_scale, qm_scale):
    x = x_ref[0]
    shift = mod_ref[0, 0:1, :]
    scale = mod_ref[0, 1:2, :]
    h = x * lax.rsqrt(jnp.mean(x * x, axis=-1, keepdims=True) + NORM_EPS) * g1_ref[...]
    h = h * (1.0 + scale) + shift
    u = jnp.dot(h.astype(BF16), w_ref[...], preferred_element_type=F32)

    hy_ref[0] = u[:, IN_OFF[0]:IN_OFF[1]]

    e2 = e2_ref[...]

    def head_norm(t, g):
        sq = t * t
        hi = sq.astype(BF16)
        lo = (sq - hi.astype(F32)).astype(BF16)
        ss = (jnp.dot(hi, e2, preferred_element_type=F32) + jnp.dot(lo, e2, preferred_element_type=F32))
        return t * lax.rsqrt(ss * (1.0 / GQA_HEAD_DIM) + NORM_EPS) * g

    for i in range(GQA_HEADS * GQA_HEAD_DIM // LANES):
        t = u[:, IN_OFF[1] + LANES * i: IN_OFF[1] + LANES * (i + 1)]
        t = _rope(head_norm(t, gq_ref[...]), cg_ref, s1g_ref, s2g_ref, GQA_HEAD_DIM // 4)
        q_ref[0, :, LANES * i: LANES * (i + 1)] = (t * q_scale).astype(BF16)
    t = u[:, IN_OFF[2]:IN_OFF[3]]
    k_ref[0] = _rope(head_norm(t, gk_ref[...]), cg_ref, s1g_ref, s2g_ref, GQA_HEAD_DIM // 4).astype(BF16)
    vt_ref[0] = u[:, IN_OFF[3]:IN_OFF[4]].T.astype(BF16)

    mq = u[:, IN_OFF[4]:IN_OFF[5]]
    mqn = mq * lax.rsqrt(jnp.mean(mq * mq, axis=-1, keepdims=True) + NORM_EPS) * gmq_ref[...]
    qm = jnp.dot(mqn.astype(BF16), wq_ref[...], preferred_element_type=F32)
    for hd in range(MLA_HEADS):
        t = _rope(qm[:, LANES * hd: LANES * (hd + 1)], cm_ref, s1m_ref, s2m_ref, MLA_ROPE_DIM // 4)
        qm_ref[0, :, LANES * hd: LANES * (hd + 1)] = (t * qm_scale).astype(BF16)

    mkv = u[:, IN_OFF[5]:IN_OFF[6]]
    kvn = (mkv * lax.rsqrt(jnp.mean(mkv * mkv, axis=-1, keepdims=True) + NORM_EPS) * gmkv_ref[...]).astype(BF16)
    kpe = _rope(u[:, IN_OFF[6]:IN_OFF[6] + LANES], ck_ref, s1k_ref, s2k_ref, MLA_ROPE_DIM // 4).astype(BF16)
    kcat = jnp.concatenate([kvn, kpe], axis=1)
    km_ref[0] = jnp.dot(kcat, wkp_ref[...], preferred_element_type=F32).astype(BF16)
    vmt_ref[0] = jnp.dot(kvn, wvp_ref[...], preferred_element_type=F32).T.astype(BF16)


def _inproj(x, mod, lw, tabs):
    B, L, D = x.shape
    tm = min(512, L)
    nh = MLA_HEADS * LANES
    tab_spec = pl.BlockSpec((tm, LANES), lambda i, b: (i, 0))
    tok = lambda w: pl.BlockSpec((1, tm, w), lambda i, b: (b, i, 0))
    kern = functools.partial(_inproj_kernel,
                             q_scale=GQA_HEAD_DIM ** -0.5 * LOG2E, qm_scale=MLA_QK_DIM ** -0.5 * LOG2E)
    consts = [lw["g1"], lw["w_in"], lw["e2"], lw["gq"], lw["gk"], lw["gmq"], lw["gmkv"],
              lw["wq"], lw["wkp"], lw["wvp"]]
    out_w = [(IN_SPLITS[0], F32, False), (IN_SPLITS[1], BF16, False), (IN_SPLITS[2], BF16, False),
             (IN_SPLITS[3], BF16, True), (nh, BF16, False), (nh, BF16, False), (MLA_HEADS * MLA_V_DIM, BF16, True)]
    tok_t = lambda w: pl.BlockSpec((1, w, tm), lambda i, b: (b, 0, i))
    return pl.pallas_call(
        kern,
        grid=(L // tm, B),
        in_specs=[tok(D), pl.BlockSpec((1, N_MOD, D), lambda i, b: (b, 0, 0))]
        + [_const_spec(a.shape) for a in consts] + [tab_spec] * 9,
        out_specs=[tok_t(w) if tr else tok(w) for w, _, tr in out_w],
        out_shape=[jax.ShapeDtypeStruct((B, w, L) if tr else (B, L, w), dt) for w, dt, tr in out_w],
        compiler_params=_cparams("parallel", "parallel"),
        name="inproj",
    )(x, mod, *consts, *tabs)


def _shortconv_kernel(u_ref, w_ref, b_ref, o_ref):
    u = u_ref[0]
    L = u.shape[0]
    row = lax.broadcasted_iota(jnp.int32, u.shape, 0)
    prev = jnp.where(row == 0, 0.0, pltpu.roll(u, 1, 0))
    nxt = jnp.where(row == L - 1, 0.0, pltpu.roll(u, L - 1, 0))
    y = prev * w_ref[0, 0:1, :] + u * w_ref[0, 1:2, :] + nxt * w_ref[0, 2:3, :] + b_ref[0]
    o_ref[0, 0] = y.T


def _shortconv_t(hy, conv_w, conv_b):
    B, L, _ = hy.shape
    w = jnp.transpose(conv_w.reshape(3, 3, HY_W), (1, 0, 2))
    b = conv_b.reshape(3, 1, HY_W)
    return pl.pallas_call(
        _shortconv_kernel,
        grid=(B, 3),
        in_specs=[
            pl.BlockSpec((1, L, HY_W), lambda bb, p: (bb, 0, p)),
            pl.BlockSpec((1, 3, HY_W), lambda bb, p: (p, 0, 0)),
            pl.BlockSpec((1, 1, HY_W), lambda bb, p: (p, 0, 0)),
        ],
        out_specs=pl.BlockSpec((1, 1, HY_W, L), lambda bb, p: (p, bb, 0, 0)),
        out_shape=jax.ShapeDtypeStruct((3, B, HY_W, L), F32),
        compiler_params=_cparams("parallel", "parallel"),
        name="hyena_shortconv",
    )(hy, w, b)


def _longconv_kernel(bias_ref, kf_ref, u_ref, o_ref, s0_ref, s1_ref, acc_ref, x_ref, y_ref, *, nblk):
    g = pl.program_id(0)
    B, CH = u_ref.shape[1], u_ref.shape[2]
    P = s0_ref.shape[1]
    for p in range(3):
        x_ref[p] = jnp.swapaxes(u_ref[p], 0, 1)

    def to_rows(z):
        return jnp.concatenate([z[:, j * CONV_BLK:(j + 1) * CONV_BLK] for j in range(nblk)], axis=0)

    def conv(zin, o, c, s_ref):
        krow = kf_ref[o, pl.ds(c, 1), :]
        s_ref[...] = pltpu.roll(jnp.broadcast_to(krow, (CONV_BLK, P)), 0, 1,
                                stride=1, stride_axis=0).astype(BF16)
        zb = zin.astype(BF16)
        acc_ref[...] = jnp.zeros_like(acc_ref)
        for d in range(-(nblk - 1), nblk):
            j0 = max(0, -d)
            n = nblk - abs(d)
            i0 = j0 + d
            col = (d * CONV_BLK) % P
            acc_ref[B * i0: B * (i0 + n), :] += jnp.dot(
                zb[B * j0: B * (j0 + n), :], s_ref[:, col: col + CONV_BLK],
                preferred_element_type=F32)
        return acc_ref[...] + zin * bias_ref[o, g * CH + c]

    def channel(c, carry):
        z1 = to_rows(x_ref[1, c]) * conv(to_rows(x_ref[0, c]), 0, c, s0_ref)
        y = to_rows(x_ref[2, c]) * conv(z1, 1, c, s1_ref)
        y_ref[c] = jnp.concatenate([y[B * i: B * (i + 1)] for i in range(nblk)], axis=1)
        return carry

    lax.fori_loop(0, CH, channel, 0)
    o_ref[...] = jnp.swapaxes(y_ref[...], 0, 1)


def _longconv(ut, kf, hy_bias):
    _, B, C, L = ut.shape
    P = kf.shape[-1]
    nblk = L // CONV_BLK
    return pl.pallas_call(
        functools.partial(_longconv_kernel, nblk=nblk),
        grid=(C // CONV_CH,),
        in_specs=[
            pl.BlockSpec(memory_space=pltpu.SMEM),
            pl.BlockSpec((HY_ORDER, CONV_CH, P), lambda g: (0, g, 0)),
            pl.BlockSpec((3, B, CONV_CH, L), lambda g: (0, 0, g, 0)),
        ],
        out_specs=pl.BlockSpec((B, CONV_CH, L), lambda g: (0, g, 0)),
        out_shape=jax.ShapeDtypeStruct((B, C, L), F32),
        scratch_shapes=[pltpu.VMEM((CONV_BLK, P), BF16), pltpu.VMEM((CONV_BLK, P), BF16),
                        pltpu.VMEM((nblk * B, CONV_BLK), F32),
                        pltpu.VMEM((3, CONV_CH, B, L), F32), pltpu.VMEM((CONV_CH, B, L), F32)],
        compiler_params=_cparams("parallel"),
        name="hyena_longconv",
    )(hy_bias, kf, ut)


def _hyena(hy, lw, kf):
    ut = _shortconv_t(hy, lw["conv_w"], lw["conv_b"])
    return _longconv(ut, kf, lw["hy_bias"])


SUBLANES = 8
ONES_ROWS = 16


def _attn_units_step(qs, k_refs, vt_refs, s_new, s_old, m_new, m_old, koffs, voffs, vrows, tk):
    nu = len(qs)
    M = qs[0].shape[0]
    dn = (((1,), (1,)), ((), ()))
    row = 0
    for k_ref in k_refs:
        n = k_ref.shape[1]
        for u in range(nu):
            s_new[u, row:row + n, :] = lax.dot_general(k_ref[0, :, koffs[u]:koffs[u] + LANES], qs[u], dn,
                                                       preferred_element_type=F32)
        row += n
    mo = [m_old[u, 0:1, :] for u in range(nu)]
    mx = [jnp.full((SUBLANES, M), -jnp.inf, F32) for _ in range(nu)]
    acc = [jnp.zeros((vrows + ONES_ROWS, M), F32) for _ in range(nu)]
    ones = jnp.ones((ONES_ROWS, tk), BF16)
    row = 0
    for vt_ref in vt_refs:
        for c in range(vt_ref.shape[2] // tk):
            rows = slice(row, row + tk)
            row += tk
            for u in range(nu):
                p = jnp.exp2(s_old[u, rows, :] - mo[u]).astype(BF16)
                vt = jnp.concatenate([vt_ref[0, voffs[u]:voffs[u] + vrows, c * tk:(c + 1) * tk], ones], axis=0)
                acc[u] = acc[u] + jnp.dot(vt, p, preferred_element_type=F32)
                sn = s_new[u, rows, :]
                for r in range(tk // SUBLANES):
                    mx[u] = jnp.maximum(mx[u], sn[SUBLANES * r: SUBLANES * (r + 1), :])
    for u in range(nu):
        m_new[u] = jnp.broadcast_to(jnp.max(mx[u], axis=0, keepdims=True), (SUBLANES, M))
    return acc


def _attn_pipeline(step, s0_ref, s1_ref, m0_ref, m1_ref):
    g = pl.program_id(0)

    @pl.when(g == 0)
    def _():
        s1_ref[...] = jnp.zeros_like(s1_ref)
        m1_ref[...] = jnp.zeros_like(m1_ref)

    @pl.when(g % 2 == 0)
    def _():
        step(s0_ref, s1_ref, m0_ref, m1_ref)

    @pl.when(g % 2 == 1)
    def _():
        step(s1_ref, s0_ref, m1_ref, m0_ref)


def _gqa_kernel(*refs, nseg, tk):
    q_ref, k_refs, v_refs = refs[0], refs[1:1 + nseg], refs[1 + nseg:1 + 2 * nseg]
    o_ref, s0_ref, s1_ref, m0_ref, m1_ref = refs[1 + 2 * nseg:]
    tq = q_ref.shape[1]
    ntile = q_ref.shape[2] // LANES
    vrows = GQA_KV_HEADS * GQA_HEAD_DIM

    def step(s_new, s_old, m_new, m_old):
        low = lax.broadcasted_iota(jnp.int32, (tq, LANES), 1) < GQA_HEAD_DIM
        zero = jnp.zeros((tq, LANES), BF16)
        qs = []
        for t in range(ntile):
            qt = q_ref[0, :, LANES * t: LANES * (t + 1)]
            qs.append(jnp.where(low, qt, zero))
            qs.append(jnp.where(low, zero, qt))
        (acc,) = _attn_units_step([jnp.concatenate(qs, axis=0)], k_refs, v_refs, s_new, s_old, m_new, m_old,
                                  (0,), (0,), vrows, tk)
        o = acc[:vrows, :] * (1.0 / acc[vrows:vrows + 1, :])
        top = lax.broadcasted_iota(jnp.int32, (vrows, tq), 0) < GQA_HEAD_DIM
        for t in range(ntile):
            oa = o[:, (2 * t) * tq:(2 * t + 1) * tq]
            ob = o[:, (2 * t + 1) * tq:(2 * t + 2) * tq]
            o_ref[0, :, LANES * t: LANES * (t + 1)] = jnp.where(top, oa, ob).T.astype(BF16)

    _attn_pipeline(step, s0_ref, s1_ref, m0_ref, m1_ref)


def _gqa_attention(q, ks, vts):
    B, L, W = q.shape
    T = sum(k.shape[1] for k in ks)
    tq = min(128, L)
    nq = L // tq
    M = GQA_HEADS * tq
    last = B * nq - 1
    new = lambda g: jnp.minimum(g, last)
    old = lambda g: jnp.maximum(g - 1, 0)
    return pl.pallas_call(
        functools.partial(_gqa_kernel, nseg=len(ks), tk=256),
        grid=(B * nq + 1,),
        in_specs=[pl.BlockSpec((1, tq, W), lambda g: (new(g) // nq, new(g) % nq, 0))]
        + [pl.BlockSpec((1, k.shape[1], LANES), lambda g: (new(g) // nq, 0, 0)) for k in ks]
        + [pl.BlockSpec((1, LANES, v.shape[2]), lambda g: (old(g) // nq, 0, 0)) for v in vts],
        out_specs=pl.BlockSpec((1, tq, W), lambda g: (old(g) // nq, old(g) % nq, 0)),
        out_shape=jax.ShapeDtypeStruct((B, L, W), BF16),
        scratch_shapes=[pltpu.VMEM((1, T, M), F32), pltpu.VMEM((1, T, M), F32),
                        pltpu.VMEM((1, SUBLANES, M), F32), pltpu.VMEM((1, SUBLANES, M), F32)],
        compiler_params=_cparams("arbitrary"),
        name="gqa_attention",
    )(q, *ks, *vts)


def _mla_kernel(*refs, nseg, tk):
    q_ref, k_refs, v_refs = refs[0], refs[1:1 + nseg], refs[1 + nseg:1 + 2 * nseg]
    o_ref, s0_ref, s1_ref, m0_ref, m1_ref = refs[1 + 2 * nseg:]

    def step(s_new, s_old, m_new, m_old):
        qs = [q_ref[0, :, 0:LANES], q_ref[0, :, LANES:2 * LANES]]
        accs = _attn_units_step(qs, k_refs, v_refs, s_new, s_old, m_new, m_old,
                                (0, LANES), (0, MLA_V_DIM), MLA_V_DIM, tk)
        o = jnp.concatenate([a[:MLA_V_DIM, :] * (1.0 / a[MLA_V_DIM:MLA_V_DIM + 1, :]) for a in accs], axis=0)
        o_ref[0] = o.T.astype(BF16)

    _attn_pipeline(step, s0_ref, s1_ref, m0_ref, m1_ref)


def _mla_attention(q, ks, vts):
    B, L, W = q.shape
    T = sum(k.shape[1] for k in ks)
    npair = W // (2 * LANES)
    tq = min(256, L)
    nq = L // tq
    last = B * npair * nq - 1
    new = lambda g: jnp.minimum(g, last)
    old = lambda g: jnp.maximum(g - 1, 0)
    bidx = lambda u: u // (npair * nq)
    pidx = lambda u: (u // nq) % npair
    return pl.pallas_call(
        functools.partial(_mla_kernel, nseg=len(ks), tk=256),
        grid=(B * npair * nq + 1,),
        in_specs=[pl.BlockSpec((1, tq, 2 * LANES), lambda g: (bidx(new(g)), new(g) % nq, pidx(new(g))))]
        + [pl.BlockSpec((1, k.shape[1], 2 * LANES), lambda g: (bidx(new(g)), 0, pidx(new(g)))) for k in ks]
        + [pl.BlockSpec((1, 2 * MLA_V_DIM, v.shape[2]), lambda g: (bidx(old(g)), pidx(old(g)), 0)) for v in vts],
        out_specs=pl.BlockSpec((1, tq, LANES), lambda g: (bidx(old(g)), old(g) % nq, pidx(old(g)))),
        out_shape=jax.ShapeDtypeStruct((B, L, npair * LANES), BF16),
        scratch_shapes=[pltpu.VMEM((2, T, tq), F32), pltpu.VMEM((2, T, tq), F32),
                        pltpu.VMEM((2, SUBLANES, tq), F32), pltpu.VMEM((2, SUBLANES, tq), F32)],
        compiler_params=_cparams("arbitrary"),
        name="mla_attention",
    )(q, *ks, *vts)


def _outffn_kernel(x_ref, yh_ref, yg_ref, ym_ref, mod_ref, g2_ref, wo_ref, w1_ref, w3_ref, w2_ref,
                   gf_ref, o_ref, *, hchunk, final):
    x = x_ref[0]
    mix = jnp.concatenate([yh_ref[0].T.astype(BF16), yg_ref[0], ym_ref[0]], axis=1)
    y = jnp.dot(mix, wo_ref[...], preferred_element_type=F32)
    x1 = x + mod_ref[0, 2:3, :] * y
    h = x1 * lax.rsqrt(jnp.mean(x1 * x1, axis=-1, keepdims=True) + NORM_EPS) * g2_ref[...]
    h = (h * (1.0 + mod_ref[0, 4:5, :]) + mod_ref[0, 3:4, :]).astype(BF16)
    f = jnp.zeros_like(x)
    hidden = w1_ref.shape[1]
    for lo in range(0, hidden, hchunk):
        cols = slice(lo, min(lo + hchunk, hidden))
        a = jnp.dot(h, w1_ref[:, cols], preferred_element_type=F32)
        b = jnp.dot(h, w3_ref[:, cols], preferred_element_type=F32)
        g = (a * jax.nn.sigmoid(a) * b).astype(BF16)
        f = f + jnp.dot(g, w2_ref[cols, :], preferred_element_type=F32)
    x2 = x1 + mod_ref[0, 5:6, :] * f
    if final:
        x2 = x2 * lax.rsqrt(jnp.mean(x2 * x2, axis=-1, keepdims=True) + NORM_EPS) * gf_ref[...]
    o_ref[0] = x2


def _outffn(x, yh_t, yg, ym, mod, lw, final_g, final):
    B, L, D = x.shape
    tm = min(512, L)
    hidden = lw["w1"].shape[1]
    consts = [lw["g2"], lw["w_out"], lw["w1"], lw["w3"], lw["w2"], final_g]
    kern = functools.partial(_outffn_kernel, hchunk=6 * MXU_TILE, final=final)
    tok = lambda w: pl.BlockSpec((1, tm, w), lambda b, i: (b, i, 0))
    return pl.pallas_call(
        kern,
        grid=(B, L // tm),
        in_specs=[tok(D), pl.BlockSpec((1, HY_W, tm), lambda b, i: (b, 0, i)),
                  tok(yg.shape[2]), tok(ym.shape[2]),
                  pl.BlockSpec((1, N_MOD, D), lambda b, i: (b, 0, 0))]
        + [_const_spec(a.shape) for a in consts],
        out_specs=tok(D),
        out_shape=jax.ShapeDtypeStruct((B, L, D), F32),
        compiler_params=_cparams("parallel", "parallel"),
        name="outffn",
    )(x, yh_t, yg, ym, mod, *consts)


def _rope_tables(L, with_pos):
    lane = np.arange(LANES)

    def pattern(dim, off, width):
        loc = (lane - off) % dim
        half = dim // 2
        active = (lane >= off) & (lane < off + width)
        use_col = loc >= half
        fi = (loc % half) % (half // 2)
        first = (loc % half) < (half // 2)
        return active, use_col, fi, first, half

    def tables(dim, off, width):
        active, use_col, fi, first, half = pattern(dim, off, width)
        if not with_pos:
            one = jnp.ones((L, LANES), F32)
            zero = jnp.zeros((L, LANES), F32)
            return [one, zero, zero]
        row = jnp.repeat(jnp.arange(L // GRID_W, dtype=jnp.int32), GRID_W).astype(F32)
        col = jnp.tile(jnp.arange(GRID_W, dtype=jnp.int32), L // GRID_W).astype(F32)
        inv = ROPE_THETA ** (-jnp.arange(0, half, 2, dtype=F32) / half)
        pos = jnp.where(jnp.asarray(use_col)[None, :], col[:, None], row[:, None])
        ang = pos * inv[jnp.asarray(fi)][None, :]
        act = jnp.asarray(active)[None, :]
        fst = jnp.asarray(first)[None, :]
        cos = jnp.where(act, jnp.cos(ang), 1.0)
        sin = jnp.sin(ang)
        s1 = jnp.where(act & fst, -sin, 0.0)
        s2 = jnp.where(act & ~fst, sin, 0.0)
        return [cos, s1, s2]

    return (tables(GQA_HEAD_DIM, 0, LANES) + tables(MLA_ROPE_DIM, MLA_NOPE_DIM, MLA_ROPE_DIM)
            + tables(MLA_ROPE_DIM, 0, MLA_ROPE_DIM))


def _gqa_tile_order():
    per = GQA_HEADS // GQA_KV_HEADS
    return [h for t in range(per) for h in (t, per + t)]


def _layer_weights(l, p):
    hd = GQA_HEAD_DIM
    order = _gqa_tile_order()
    qperm = np.concatenate([np.arange(hd) + hd * h for h in order])
    w_in = p["w_in"][l]
    w_in = jnp.concatenate([w_in[:, :IN_OFF[1]], w_in[:, IN_OFF[1]:IN_OFF[2]][:, qperm], w_in[:, IN_OFF[2]:]], axis=1)
    w_in = jnp.pad(w_in, ((0, 0), (0, IN_WIDTH_PAD - w_in.shape[1]))).astype(BF16)

    wq = p["mla_w_uq"][l].reshape(-1, MLA_HEADS, MLA_QK_DIM)
    wq = jnp.pad(wq, ((0, 0), (0, 0), (0, LANES - MLA_QK_DIM))).reshape(-1, MLA_HEADS * LANES).astype(BF16)

    wkv = p["mla_w_ukv"][l].reshape(-1, MLA_HEADS, MLA_NOPE_DIM + MLA_V_DIM)
    wk = jnp.pad(wkv[:, :, :MLA_NOPE_DIM], ((0, 0), (0, 0), (0, LANES - MLA_NOPE_DIM)))
    place = np.zeros((LANES, MLA_HEADS, LANES), np.float32)
    for h in range(MLA_HEADS):
        place[np.arange(MLA_ROPE_DIM), h, MLA_NOPE_DIM + np.arange(MLA_ROPE_DIM)] = 1.0
    wkp = jnp.concatenate([wk, jnp.asarray(place)], axis=0).reshape(-1, MLA_HEADS * LANES).astype(BF16)
    wvp = wkv[:, :, MLA_NOPE_DIM:].reshape(-1, MLA_HEADS * MLA_V_DIM).astype(BF16)

    w_out = p["w_out"][l]
    g0 = HY_W
    g1 = g0 + GQA_HEADS * hd
    w_out = jnp.concatenate([w_out[:g0], w_out[g0:g1][qperm], w_out[g1:]], axis=0).astype(BF16)

    e2 = np.kron(np.eye(LANES // hd, dtype=np.float32), np.ones((hd, hd), np.float32))
    two = lambda g: jnp.tile(g, LANES // hd)[None, :]
    return {
        "g1": p["norm1_g"][l][None, :], "g2": p["norm2_g"][l][None, :],
        "w_in": w_in, "e2": jnp.asarray(e2, BF16),
        "gq": two(p["gqa_q_g"][l]), "gk": two(p["gqa_k_g"][l]),
        "gmq": p["mla_q_g"][l][None, :], "gmkv": p["mla_kv_g"][l][None, :],
        "wq": wq, "wkp": wkp, "wvp": wvp, "w_out": w_out,
        "w1": p["ffn_w1"][l].astype(BF16), "w3": p["ffn_w3"][l].astype(BF16), "w2": p["ffn_w2"][l].astype(BF16),
        "conv_w": p["hy_conv_w"][l], "conv_b": p["hy_conv_b"][l], "hy_bias": p["hy_bias"][l],
    }


def kernel(x, c, ctx, c_ctx, mod_w, mod_b, norm1_g, norm2_g, w_in, hy_conv_w, hy_conv_b, hy_filt_w1, hy_filt_b1, hy_filt_w2, hy_filt_b2, hy_filt_w3, hy_filt_freq, hy_bias, gqa_q_g, gqa_k_g, mla_q_g, mla_kv_g, mla_w_uq, mla_w_ukv, w_out, ffn_w1, ffn_w3, ffn_w2, final_g):
    p = dict(norm1_g=norm1_g, norm2_g=norm2_g, w_in=w_in, hy_conv_w=hy_conv_w, hy_conv_b=hy_conv_b,
             hy_bias=hy_bias, gqa_q_g=gqa_q_g, gqa_k_g=gqa_k_g, mla_q_g=mla_q_g, mla_kv_g=mla_kv_g,
             mla_w_uq=mla_w_uq, mla_w_ukv=mla_w_ukv, w_out=w_out, ffn_w1=ffn_w1, ffn_w3=ffn_w3, ffn_w2=ffn_w2)
    B, L, D = x.shape
    Lc = ctx.shape[1]
    depth = mod_w.shape[0]

    pad_rows = (-(B + 1)) % 8
    c_all = jnp.concatenate([c, c_ctx[None, :], jnp.zeros((pad_rows, D), F32)], axis=0)
    mod = _modulation(c_all, mod_w, mod_b).reshape(depth, -1, N_MOD, D)

    tabs_lat = _rope_tables(L, True)
    tabs_ctx = _rope_tables(Lc, False)
    final_g2 = final_g[None, :]

    x_lat, x_ctx = x, ctx
    for l in range(depth):
        last = l == depth - 1
        lw = _layer_weights(l, p)
        filt = (hy_filt_w1[l], hy_filt_b1[l], hy_filt_w2[l], hy_filt_b2[l], hy_filt_w3[l], hy_filt_freq[l])
        mod_lat = mod[l, :B]
        mod_ctx = jnp.broadcast_to(mod[l, B][None], (B, N_MOD, D))

        hy_c, q_c, k_c, vt_c, qm_c, km_c, vmt_c = _inproj(x_ctx, mod_ctx, lw, tabs_ctx)
        hy, q, k, vt, qm, km, vmt = _inproj(x_lat, mod_lat, lw, tabs_lat)

        if not last:
            yh_c = _hyena(hy_c, lw, _hyena_filters(Lc, *filt))
            yg_c = _gqa_attention(q_c, [k_c], [vt_c])
            ym_c = _mla_attention(qm_c, [km_c], [vmt_c])
            x_ctx_next = _outffn(x_ctx, yh_c, yg_c, ym_c, mod_ctx, lw, final_g2, False)

        yh = _hyena(hy, lw, _hyena_filters(L, *filt))
        yg = _gqa_attention(q, [k, k_c], [vt, vt_c])
        ym = _mla_attention(qm, [km, km_c], [vmt, vmt_c])
        x_lat = _outffn(x_lat, yh, yg, ym, mod_lat, lw, final_g2, last)
        if not last:
            x_ctx = x_ctx_next
    return x_lat
```

```python
import functools
import math

import jax
import jax.numpy as jnp
import numpy as np
from jax import lax
from jax.experimental import pallas as pl
from jax.experimental.pallas import tpu as pltpu

F32 = jnp.float32
BF16 = jnp.bfloat16

N_MOD = 6
GRID_W = 64
NORM_EPS = 1e-6
ROPE_THETA = 10000.0
HY_W = 256
HY_ORDER = 2
HY_BANDS = 16
HY_TARGET = 1e-2
HY_FAST_PCT = 0.3
HY_SLOW_PCT = 1.5
GQA_HEADS = 6
GQA_KV_HEADS = 2
GQA_HEAD_DIM = 64
MLA_HEADS = 6
MLA_NOPE_DIM = 64
MLA_ROPE_DIM = 32
MLA_V_DIM = 64
MLA_QK_DIM = MLA_NOPE_DIM + MLA_ROPE_DIM
IN_SPLITS = (768, 384, 128, 128, 256, 128, 32)
IN_OFF = tuple(int(v) for v in np.cumsum((0,) + IN_SPLITS))
IN_WIDTH_PAD = 1920

LANES = 128
MXU_TILE = 256
MLA_PE_LANES = -(-MLA_HEADS * MLA_ROPE_DIM // LANES) * LANES
VMEM_LIMIT = 56 * 1024 * 1024
CONV_BLK = 256
CONV_CH = 8

LOG2E = 1.4426950408889634
HIGHEST = lax.Precision.HIGHEST


def _cparams(*sem):
    return pltpu.CompilerParams(dimension_semantics=sem, vmem_limit_bytes=VMEM_LIMIT)


def _const_spec(shape):
    zeros = (0,) * len(shape)
    return pl.BlockSpec(shape, lambda *_: zeros, pipeline_mode=pl.Buffered(1))


def _mod_kernel(c_ref, w_ref, b_ref, o_ref):
    c = c_ref[...]
    sc = c * jax.nn.sigmoid(c)
    o_ref[0] = jnp.dot(sc, w_ref[0], precision=HIGHEST, preferred_element_type=F32) + b_ref[0]


def _modulation(c_all, mod_w, mod_b):
    depth, d, n = mod_w.shape
    rows = c_all.shape[0]
    tn = 1536
    return pl.pallas_call(
        _mod_kernel,
        grid=(depth, n // tn),
        in_specs=[
            pl.BlockSpec((rows, d), lambda l, j: (0, 0)),
            pl.BlockSpec((1, d, tn), lambda l, j: (l, 0, j)),
            pl.BlockSpec((1, 1, tn), lambda l, j: (l, 0, j)),
        ],
        out_specs=pl.BlockSpec((1, rows, tn), lambda l, j: (l, 0, j)),
        out_shape=jax.ShapeDtypeStruct((depth, rows, n), F32),
        compiler_params=_cparams("parallel", "parallel"),
        name="modulation",
    )(c_all, mod_w, mod_b.reshape(depth, 1, n))


def _filter_kernel(zt_ref, w1t_ref, b1_ref, w2t_ref, b2_ref, fr_ref, w3f_ref, w3b_ref,
                   trow_ref, sel_ref, delta_ref, o_ref, h_ref):
    @pl.when(pl.program_id(0) == 0)
    def _():
        fr = fr_ref[...]
        h = jnp.sin(fr * (jnp.dot(w1t_ref[...], zt_ref[...], precision=HIGHEST,
                                  preferred_element_type=F32) + b1_ref[...]))
        h_ref[...] = jnp.sin(fr * (jnp.dot(w2t_ref[...], h, precision=HIGHEST,
                                           preferred_element_type=F32) + b2_ref[...]))

    h = h_ref[...]
    fwd = jnp.dot(w3f_ref[...], h, precision=HIGHEST, preferred_element_type=F32)
    bwd = jnp.dot(w3b_ref[...], h, precision=HIGHEST, preferred_element_type=F32)
    sel = sel_ref[...]
    k = jnp.where(sel > 0.0, fwd, jnp.where(sel < 0.0, bwd, 0.0))
    k = k * jnp.exp(-delta_ref[...] * trow_ref[...])
    nrm = jnp.sum(jnp.abs(k), axis=1, keepdims=True)
    o_ref[0] = k / nrm


def _filter_features(L):
    t = np.linspace(0.0, 1.0, L, dtype=np.float32)
    wpos = np.float32(2.0 * math.pi) * np.arange(L, dtype=np.float32) / np.float32(L)
    f = np.linspace(1e-4, HY_BANDS - 1, HY_BANDS, dtype=np.float32)
    ang = (f[None, :] * wpos[:, None]).astype(np.float64)
    z = np.concatenate([t[:, None], np.cos(ang), -np.sin(ang)], axis=1).astype(np.float32)
    n = np.arange(2 * L)
    lag = np.where(n <= L, np.minimum(n, L - 1), 2 * L - n)
    sel = np.where(n < L, 1.0, np.where(n > L, -1.0, 0.0)).astype(np.float32)
    zt = np.zeros((LANES, 2 * L), np.float32)
    zt[:z.shape[1]] = z[lag].T
    return zt, t[lag][None, :], sel[None, :]


def _hyena_filters(L, w1, b1, w2, b2, w3, freq):
    P = 2 * L
    zt, trow, sel = _filter_features(L)
    nf = w1.shape[1]
    w1t = jnp.pad(w1, ((0, LANES - w1.shape[0]), (0, 0))).T
    col = lambda a: a[:, None]
    deltas = np.abs(np.linspace(math.log(HY_TARGET) / HY_FAST_PCT, math.log(HY_TARGET) / HY_SLOW_PCT,
                                HY_W, dtype=np.float32))[:, None]
    halves = HY_W // LANES
    cb = 2 * halves
    whole = lambda shape: pl.BlockSpec(shape, lambda s: (0, 0))
    return pl.pallas_call(
        _filter_kernel,
        grid=(HY_ORDER * halves,),
        in_specs=[
            whole((LANES, P)), whole((nf, LANES)), whole((nf, 1)), whole((nf, nf)), whole((nf, 1)),
            whole((nf, 1)),
            pl.BlockSpec((LANES, nf), lambda s: ((s // halves) * cb + s % halves, 0)),
            pl.BlockSpec((LANES, nf), lambda s: ((s // halves) * cb + halves + s % halves, 0)),
            whole((1, P)), whole((1, P)),
            pl.BlockSpec((LANES, 1), lambda s: (s % halves, 0)),
        ],
        out_specs=pl.BlockSpec((1, LANES, P), lambda s: (s // halves, s % halves, 0)),
        out_shape=jax.ShapeDtypeStruct((HY_ORDER, HY_W, P), F32),
        scratch_shapes=[pltpu.VMEM((nf, P), F32)],
        compiler_params=_cparams("arbitrary"),
        name="hyena_filters",
    )(zt, w1t, col(b1), w2.T, col(b2), col(freq), w3.T, w3.T, trow, sel, deltas)


def _rope(t, c_ref, s1_ref, s2_ref, sh):
    return (t * c_ref[...] + pltpu.roll(t, LANES - sh, 1) * s1_ref[...]
            + pltpu.roll(t, sh, 1) * s2_ref[...])


def _inproj_kernel(x_ref, mod_ref, g1_ref, w_ref, e2_ref, gq_ref, gk_ref, gmq_ref, gmkv_ref,
                   wq_ref, wabs_ref, krep_ref,
                   cg_ref, s1g_ref, s2g_ref, cm_ref, s1m_ref, s2m_ref,
                   hy_ref, q_ref, k_ref, vt_ref, qabs_ref, qpe_ref, km_ref, vmt_ref, *, q_scale, qm_scale):
    x = x_ref[0]
    shift = mod_ref[0, 0:1, :]
    scale = mod_ref[0, 1:2, :]
    h = x * lax.rsqrt(jnp.mean(x * x, axis=-1, keepdims=True) + NORM_EPS) * g1_ref[...]
    h = h * (1.0 + scale) + shift
    u = jnp.dot(h.astype(BF16), w_ref[...], preferred_element_type=F32)

    hy_ref[0] = u[:, IN_OFF[0]:IN_OFF[1]]

    e2 = e2_ref[...]

    def head_norm(t, g):
        sq = t * t
        hi = sq.astype(BF16)
        lo = (sq - hi.astype(F32)).astype(BF16)
        ss = (jnp.dot(hi, e2, preferred_element_type=F32) + jnp.dot(lo, e2, preferred_element_type=F32))
        return t * lax.rsqrt(ss * (1.0 / GQA_HEAD_DIM) + NORM_EPS) * g

    for i in range(GQA_HEADS * GQA_HEAD_DIM // LANES):
        t = u[:, IN_OFF[1] + LANES * i: IN_OFF[1] + LANES * (i + 1)]
        t = _rope(head_norm(t, gq_ref[...]), cg_ref, s1g_ref, s2g_ref, GQA_HEAD_DIM // 4)
        q_ref[0, :, LANES * i: LANES * (i + 1)] = (t * q_scale).astype(BF16)
    t = u[:, IN_OFF[2]:IN_OFF[3]]
    k_ref[0] = _rope(head_norm(t, gk_ref[...]), cg_ref, s1g_ref, s2g_ref, GQA_HEAD_DIM // 4).astype(BF16)
    vt_ref[0] = u[:, IN_OFF[3]:IN_OFF[4]].T.astype(BF16)

    mq = u[:, IN_OFF[4]:IN_OFF[5]]
    mqn = mq * lax.rsqrt(jnp.mean(mq * mq, axis=-1, keepdims=True) + NORM_EPS) * gmq_ref[...]
    qm = jnp.dot(mqn.astype(BF16), wq_ref[...], preferred_element_type=F32)
    n_nope = MLA_HEADS * MLA_NOPE_DIM
    qabs = jnp.dot(qm[:, :n_nope].astype(BF16), wabs_ref[...], preferred_element_type=F32)
    qabs_ref[0] = (qabs * qm_scale).astype(BF16)
    for i in range(qpe_ref.shape[2] // LANES):
        t = _rope(qm[:, n_nope + LANES * i: n_nope + LANES * (i + 1)], cm_ref, s1m_ref, s2m_ref, MLA_ROPE_DIM // 4)
        qpe_ref[0, :, LANES * i: LANES * (i + 1)] = (t * qm_scale).astype(BF16)

    mkv = u[:, IN_OFF[5]:IN_OFF[6]]
    kvn = mkv * lax.rsqrt(jnp.mean(mkv * mkv, axis=-1, keepdims=True) + NORM_EPS) * gmkv_ref[...]
    kpe = _rope(u[:, IN_OFF[6]:IN_OFF[6] + LANES], cm_ref, s1m_ref, s2m_ref, MLA_ROPE_DIM // 4).astype(BF16)
    kpe4 = jnp.dot(kpe, krep_ref[...], preferred_element_type=F32).astype(BF16)
    km_ref[0] = jnp.concatenate([kvn.astype(BF16), kpe4], axis=1)
    vmt_ref[0] = kvn.T.astype(BF16)


def _inproj(x, mod, lw, tabs):
    B, L, D = x.shape
    tm = min(512, L)
    tab_spec = pl.BlockSpec((tm, LANES), lambda i, b: (i, 0))
    tok = lambda w: pl.BlockSpec((1, tm, w), lambda i, b: (b, i, 0))
    kern = functools.partial(_inproj_kernel,
                             q_scale=GQA_HEAD_DIM ** -0.5 * LOG2E, qm_scale=MLA_QK_DIM ** -0.5 * LOG2E)
    consts = [lw["g1"], lw["w_in"], lw["e2"], lw["gq"], lw["gk"], lw["gmq"], lw["gmkv"],
              lw["wq"], lw["wabs"], lw["krep"]]
    latent = lw["wabs"].shape[1] // MLA_HEADS
    out_w = [(IN_SPLITS[0], F32, False), (IN_SPLITS[1], BF16, False), (IN_SPLITS[2], BF16, False),
             (IN_SPLITS[3], BF16, True), (MLA_HEADS * latent, BF16, False), (MLA_PE_LANES, BF16, False),
             (latent + LANES, BF16, False), (latent, BF16, True)]
    tok_t = lambda w: pl.BlockSpec((1, w, tm), lambda i, b: (b, 0, i))
    return pl.pallas_call(
        kern,
        grid=(L // tm, B),
        in_specs=[tok(D), pl.BlockSpec((1, N_MOD, D), lambda i, b: (b, 0, 0))]
        + [_const_spec(a.shape) for a in consts] + [tab_spec] * len(tabs),
        out_specs=[tok_t(w) if tr else tok(w) for w, _, tr in out_w],
        out_shape=[jax.ShapeDtypeStruct((B, w, L) if tr else (B, L, w), dt) for w, dt, tr in out_w],
        compiler_params=_cparams("parallel", "parallel"),
        name="inproj",
    )(x, mod, *consts, *tabs)


def _shortconv_kernel(u_ref, w_ref, b_ref, o_ref):
    u = u_ref[0]
    L = u.shape[0]
    row = lax.broadcasted_iota(jnp.int32, u.shape, 0)
    prev = jnp.where(row == 0, 0.0, pltpu.roll(u, 1, 0))
    nxt = jnp.where(row == L - 1, 0.0, pltpu.roll(u, L - 1, 0))
    y = prev * w_ref[0, 0:1, :] + u * w_ref[0, 1:2, :] + nxt * w_ref[0, 2:3, :] + b_ref[0]
    o_ref[0, 0] = y.T


def _shortconv_t(hy, conv_w, conv_b):
    B, L, _ = hy.shape
    w = jnp.transpose(conv_w.reshape(3, 3, HY_W), (1, 0, 2))
    b = conv_b.reshape(3, 1, HY_W)
    return pl.pallas_call(
        _shortconv_kernel,
        grid=(B, 3),
        in_specs=[
            pl.BlockSpec((1, L, HY_W), lambda bb, p: (bb, 0, p)),
            pl.BlockSpec((1, 3, HY_W), lambda bb, p: (p, 0, 0)),
            pl.BlockSpec((1, 1, HY_W), lambda bb, p: (p, 0, 0)),
        ],
        out_specs=pl.BlockSpec((1, 1, HY_W, L), lambda bb, p: (p, bb, 0, 0)),
        out_shape=jax.ShapeDtypeStruct((3, B, HY_W, L), F32),
        compiler_params=_cparams("parallel", "parallel"),
        name="hyena_shortconv",
    )(hy, w, b)


def _longconv_kernel(bias_ref, kf_ref, u_ref, o_ref, s0_ref, s1_ref, acc_ref, x_ref, y_ref, *, nblk):
    g = pl.program_id(0)
    B, CH = u_ref.shape[1], u_ref.shape[2]
    P = s0_ref.shape[1]
    for p in range(3):
        x_ref[p] = jnp.swapaxes(u_ref[p], 0, 1)

    def to_rows(z):
        return jnp.concatenate([z[:, j * CONV_BLK:(j + 1) * CONV_BLK] for j in range(nblk)], axis=0)

    def conv(zin, o, c, s_ref):
        krow = kf_ref[o, pl.ds(c, 1), :]
        s_ref[...] = pltpu.roll(jnp.broadcast_to(krow, (CONV_BLK, P)), 0, 1,
                                stride=1, stride_axis=0).astype(BF16)
        zb = zin.astype(BF16)
        acc_ref[...] = jnp.zeros_like(acc_ref)
        for d in range(-(nblk - 1), nblk):
            j0 = max(0, -d)
            n = nblk - abs(d)
            i0 = j0 + d
            col = (d * CONV_BLK) % P
            acc_ref[B * i0: B * (i0 + n), :] += jnp.dot(
                zb[B * j0: B * (j0 + n), :], s_ref[:, col: col + CONV_BLK],
                preferred_element_type=F32)
        return acc_ref[...] + zin * bias_ref[o, g * CH + c]

    def channel(c, carry):
        z1 = to_rows(x_ref[1, c]) * conv(to_rows(x_ref[0, c]), 0, c, s0_ref)
        y = to_rows(x_ref[2, c]) * conv(z1, 1, c, s1_ref)
        y_ref[c] = jnp.concatenate([y[B * i: B * (i + 1)] for i in range(nblk)], axis=1)
        return carry

    lax.fori_loop(0, CH, channel, 0)
    o_ref[...] = jnp.swapaxes(y_ref[...], 0, 1)


def _longconv(ut, kf, hy_bias):
    _, B, C, L = ut.shape
    P = kf.shape[-1]
    nblk = L // CONV_BLK
    return pl.pallas_call(
        functools.partial(_longconv_kernel, nblk=nblk),
        grid=(C // CONV_CH,),
        in_specs=[
            pl.BlockSpec(memory_space=pltpu.SMEM),
            pl.BlockSpec((HY_ORDER, CONV_CH, P), lambda g: (0, g, 0)),
            pl.BlockSpec((3, B, CONV_CH, L), lambda g: (0, 0, g, 0)),
        ],
        out_specs=pl.BlockSpec((B, CONV_CH, L), lambda g: (0, g, 0)),
        out_shape=jax.ShapeDtypeStruct((B, C, L), F32),
        scratch_shapes=[pltpu.VMEM((CONV_BLK, P), BF16), pltpu.VMEM((CONV_BLK, P), BF16),
                        pltpu.VMEM((nblk * B, CONV_BLK), F32),
                        pltpu.VMEM((3, CONV_CH, B, L), F32), pltpu.VMEM((CONV_CH, B, L), F32)],
        compiler_params=_cparams("parallel"),
        name="hyena_longconv",
    )(hy_bias, kf, ut)


def _hyena(hy, lw, kf):
    ut = _shortconv_t(hy, lw["conv_w"], lw["conv_b"])
    return _longconv(ut, kf, lw["hy_bias"])


SUBLANES = 8
ONES_ROWS = 16


def _attn_units_step(qs, k_refs, vt_refs, s_new, s_old, m_new, m_old, koffs, voffs, vrows, tk):
    nu = len(qs)
    M = qs[0].shape[0]
    dn = (((1,), (1,)), ((), ()))
    row = 0
    for k_ref in k_refs:
        n = k_ref.shape[1]
        for u in range(nu):
            s_new[u, row:row + n, :] = lax.dot_general(k_ref[0, :, koffs[u]:koffs[u] + qs[u].shape[1]], qs[u], dn,
                                                       preferred_element_type=F32)
        row += n
    mo = [m_old[u, 0:1, :] for u in range(nu)]
    mx = [jnp.full((SUBLANES, M), -jnp.inf, F32) for _ in range(nu)]
    acc = [jnp.zeros((vrows + ONES_ROWS, M), F32) for _ in range(nu)]
    ones = jnp.ones((ONES_ROWS, tk), BF16)
    row = 0
    for vt_ref in vt_refs:
        for c in range(vt_ref.shape[2] // tk):
            rows = slice(row, row + tk)
            row += tk
            for u in range(nu):
                p = jnp.exp2(s_old[u, rows, :] - mo[u]).astype(BF16)
                vt = jnp.concatenate([vt_ref[0, voffs[u]:voffs[u] + vrows, c * tk:(c + 1) * tk], ones], axis=0)
                acc[u] = acc[u] + jnp.dot(vt, p, preferred_element_type=F32)
                sn = s_new[u, rows, :]
                for r in range(tk // SUBLANES):
                    mx[u] = jnp.maximum(mx[u], sn[SUBLANES * r: SUBLANES * (r + 1), :])
    for u in range(nu):
        m_new[u] = jnp.broadcast_to(jnp.max(mx[u], axis=0, keepdims=True), (SUBLANES, M))
    return acc


def _attn_pipeline(step, s0_ref, s1_ref, m0_ref, m1_ref):
    g = pl.program_id(0)

    @pl.when(g == 0)
    def _():
        s1_ref[...] = jnp.zeros_like(s1_ref)
        m1_ref[...] = jnp.zeros_like(m1_ref)

    @pl.when(g % 2 == 0)
    def _():
        step(s0_ref, s1_ref, m0_ref, m1_ref)

    @pl.when(g % 2 == 1)
    def _():
        step(s1_ref, s0_ref, m1_ref, m0_ref)


def _gqa_kernel(*refs, nseg, tk):
    q_ref, k_refs, v_refs = refs[0], refs[1:1 + nseg], refs[1 + nseg:1 + 2 * nseg]
    o_ref, s0_ref, s1_ref, m0_ref, m1_ref = refs[1 + 2 * nseg:]
    tq = q_ref.shape[1]
    ntile = q_ref.shape[2] // LANES
    vrows = GQA_KV_HEADS * GQA_HEAD_DIM

    def step(s_new, s_old, m_new, m_old):
        low = lax.broadcasted_iota(jnp.int32, (tq, LANES), 1) < GQA_HEAD_DIM
        zero = jnp.zeros((tq, LANES), BF16)
        qs = []
        for t in range(ntile):
            qt = q_ref[0, :, LANES * t: LANES * (t + 1)]
            qs.append(jnp.where(low, qt, zero))
            qs.append(jnp.where(low, zero, qt))
        (acc,) = _attn_units_step([jnp.concatenate(qs, axis=0)], k_refs, v_refs, s_new, s_old, m_new, m_old,
                                  (0,), (0,), vrows, tk)
        o = acc[:vrows, :] * (1.0 / acc[vrows:vrows + 1, :])
        top = lax.broadcasted_iota(jnp.int32, (vrows, tq), 0) < GQA_HEAD_DIM
        for t in range(ntile):
            oa = o[:, (2 * t) * tq:(2 * t + 1) * tq]
            ob = o[:, (2 * t + 1) * tq:(2 * t + 2) * tq]
            o_ref[0, :, LANES * t: LANES * (t + 1)] = jnp.where(top, oa, ob).T.astype(BF16)

    _attn_pipeline(step, s0_ref, s1_ref, m0_ref, m1_ref)


def _gqa_attention(q, ks, vts):
    B, L, W = q.shape
    T = sum(k.shape[1] for k in ks)
    tq = min(128, L)
    nq = L // tq
    M = GQA_HEADS * tq
    last = B * nq - 1
    new = lambda g: jnp.minimum(g, last)
    old = lambda g: jnp.maximum(g - 1, 0)
    return pl.pallas_call(
        functools.partial(_gqa_kernel, nseg=len(ks), tk=256),
        grid=(B * nq + 1,),
        in_specs=[pl.BlockSpec((1, tq, W), lambda g: (new(g) // nq, new(g) % nq, 0))]
        + [pl.BlockSpec((1, k.shape[1], LANES), lambda g: (new(g) // nq, 0, 0)) for k in ks]
        + [pl.BlockSpec((1, LANES, v.shape[2]), lambda g: (old(g) // nq, 0, 0)) for v in vts],
        out_specs=pl.BlockSpec((1, tq, W), lambda g: (old(g) // nq, old(g) % nq, 0)),
        out_shape=jax.ShapeDtypeStruct((B, L, W), BF16),
        scratch_shapes=[pltpu.VMEM((1, T, M), F32), pltpu.VMEM((1, T, M), F32),
                        pltpu.VMEM((1, SUBLANES, M), F32), pltpu.VMEM((1, SUBLANES, M), F32)],
        compiler_params=_cparams("arbitrary"),
        name="gqa_attention",
    )(q, *ks, *vts)


def _mla_kernel(*refs, nseg, tk):
    qabs_ref, qpe_ref, wvt_ref = refs[0:3]
    k_refs, v_refs = refs[3:3 + nseg], refs[3 + nseg:3 + 2 * nseg]
    o_ref, s0_ref, s1_ref, m0_ref, m1_ref = refs[3 + 2 * nseg:]
    tq = qabs_ref.shape[1]
    latent = v_refs[0].shape[1]
    slots = LANES // MLA_ROPE_DIM

    def step(s_new, s_old, m_new, m_old):
        lane = lax.broadcasted_iota(jnp.int32, (tq, LANES), 1)
        zero = jnp.zeros((tq, LANES), BF16)
        qs = []
        for h in range(MLA_HEADS):
            pe = qpe_ref[0, :, LANES * (h // slots): LANES * (h // slots + 1)]
            mine = (lane >= MLA_ROPE_DIM * (h % slots)) & (lane < MLA_ROPE_DIM * (h % slots + 1))
            qs.append(jnp.concatenate([qabs_ref[0, :, latent * h: latent * (h + 1)], jnp.where(mine, pe, zero)], axis=1))
        (acc,) = _attn_units_step([jnp.concatenate(qs, axis=0)], k_refs, v_refs, s_new, s_old, m_new, m_old,
                                  (0,), (0,), latent, tk)
        olat = (acc[:latent, :] * (1.0 / acc[latent:latent + 1, :])).astype(BF16)
        outs = [jnp.dot(wvt_ref[MLA_V_DIM * h: MLA_V_DIM * (h + 1), :], olat[:, tq * h: tq * (h + 1)],
                        preferred_element_type=F32) for h in range(MLA_HEADS)]
        for t in range(MLA_HEADS // 2):
            o_ref[0, :, LANES * t: LANES * (t + 1)] = jnp.concatenate(outs[2 * t: 2 * t + 2], axis=0).T.astype(BF16)

    _attn_pipeline(step, s0_ref, s1_ref, m0_ref, m1_ref)


def _mla_attention(qabs, qpe, wvt, ks, vts):
    B, L, _ = qabs.shape
    T = sum(k.shape[1] for k in ks)
    tq = min(128, L)
    nq = L // tq
    M = MLA_HEADS * tq
    last = B * nq - 1
    new = lambda g: jnp.minimum(g, last)
    old = lambda g: jnp.maximum(g - 1, 0)
    W = MLA_HEADS * MLA_V_DIM
    return pl.pallas_call(
        functools.partial(_mla_kernel, nseg=len(ks), tk=256),
        grid=(B * nq + 1,),
        in_specs=[pl.BlockSpec((1, tq, qabs.shape[2]), lambda g: (new(g) // nq, new(g) % nq, 0)),
                  pl.BlockSpec((1, tq, qpe.shape[2]), lambda g: (new(g) // nq, new(g) % nq, 0)),
                  _const_spec(wvt.shape)]
        + [pl.BlockSpec((1,) + k.shape[1:], lambda g: (new(g) // nq, 0, 0)) for k in ks]
        + [pl.BlockSpec((1,) + v.shape[1:], lambda g: (old(g) // nq, 0, 0)) for v in vts],
        out_specs=pl.BlockSpec((1, tq, W), lambda g: (old(g) // nq, old(g) % nq, 0)),
        out_shape=jax.ShapeDtypeStruct((B, L, W), BF16),
        scratch_shapes=[pltpu.VMEM((1, T, M), F32), pltpu.VMEM((1, T, M), F32),
                        pltpu.VMEM((1, SUBLANES, M), F32), pltpu.VMEM((1, SUBLANES, M), F32)],
        compiler_params=_cparams("arbitrary"),
        name="mla_attention",
    )(qabs, qpe, wvt, *ks, *vts)


def _outffn_kernel(x_ref, yh_ref, yg_ref, ym_ref, mod_ref, g2_ref, wo_ref, w1_ref, w3_ref, w2_ref,
                   gf_ref, o_ref, *, hchunk, final):
    x = x_ref[0]
    mix = jnp.concatenate([yh_ref[0].T.astype(BF16), yg_ref[0], ym_ref[0]], axis=1)
    y = jnp.dot(mix, wo_ref[...], preferred_element_type=F32)
    x1 = x + mod_ref[0, 2:3, :] * y
    h = x1 * lax.rsqrt(jnp.mean(x1 * x1, axis=-1, keepdims=True) + NORM_EPS) * g2_ref[...]
    h = (h * (1.0 + mod_ref[0, 4:5, :]) + mod_ref[0, 3:4, :]).astype(BF16)
    f = jnp.zeros_like(x)
    hidden = w1_ref.shape[1]
    for lo in range(0, hidden, hchunk):
        cols = slice(lo, min(lo + hchunk, hidden))
        a = jnp.dot(h, w1_ref[:, cols], preferred_element_type=F32)
        b = jnp.dot(h, w3_ref[:, cols], preferred_element_type=F32)
        g = (a * jax.nn.sigmoid(a) * b).astype(BF16)
        f = f + jnp.dot(g, w2_ref[cols, :], preferred_element_type=F32)
    x2 = x1 + mod_ref[0, 5:6, :] * f
    if final:
        x2 = x2 * lax.rsqrt(jnp.mean(x2 * x2, axis=-1, keepdims=True) + NORM_EPS) * gf_ref[...]
    o_ref[0] = x2


def _outffn(x, yh_t, yg, ym, mod, lw, final_g, final):
    B, L, D = x.shape
    tm = min(512, L)
    hidden = lw["w1"].shape[1]
    consts = [lw["g2"], lw["w_out"], lw["w1"], lw["w3"], lw["w2"], final_g]
    kern = functools.partial(_outffn_kernel, hchunk=6 * MXU_TILE, final=final)
    tok = lambda w: pl.BlockSpec((1, tm, w), lambda b, i: (b, i, 0))
    return pl.pallas_call(
        kern,
        grid=(B, L // tm),
        in_specs=[tok(D), pl.BlockSpec((1, HY_W, tm), lambda b, i: (b, 0, i)),
                  tok(yg.shape[2]), tok(ym.shape[2]),
                  pl.BlockSpec((1, N_MOD, D), lambda b, i: (b, 0, 0))]
        + [_const_spec(a.shape) for a in consts],
        out_specs=tok(D),
        out_shape=jax.ShapeDtypeStruct((B, L, D), F32),
        compiler_params=_cparams("parallel", "parallel"),
        name="outffn",
    )(x, yh_t, yg, ym, mod, *consts)


def _rope_tables(L, with_pos):
    lane = np.arange(LANES)

    def pattern(dim, off, width):
        loc = (lane - off) % dim
        half = dim // 2
        active = (lane >= off) & (lane < off + width)
        use_col = loc >= half
        fi = (loc % half) % (half // 2)
        first = (loc % half) < (half // 2)
        return active, use_col, fi, first, half

    def tables(dim, off, width):
        active, use_col, fi, first, half = pattern(dim, off, width)
        if not with_pos:
            one = jnp.ones((L, LANES), F32)
            zero = jnp.zeros((L, LANES), F32)
            return [one, zero, zero]
        row = jnp.repeat(jnp.arange(L // GRID_W, dtype=jnp.int32), GRID_W).astype(F32)
        col = jnp.tile(jnp.arange(GRID_W, dtype=jnp.int32), L // GRID_W).astype(F32)
        inv = ROPE_THETA ** (-jnp.arange(0, half, 2, dtype=F32) / half)
        pos = jnp.where(jnp.asarray(use_col)[None, :], col[:, None], row[:, None])
        ang = pos * inv[jnp.asarray(fi)][None, :]
        act = jnp.asarray(active)[None, :]
        fst = jnp.asarray(first)[None, :]
        cos = jnp.where(act, jnp.cos(ang), 1.0)
        sin = jnp.sin(ang)
        s1 = jnp.where(act & fst, -sin, 0.0)
        s2 = jnp.where(act & ~fst, sin, 0.0)
        return [cos, s1, s2]

    return tables(GQA_HEAD_DIM, 0, LANES) + tables(MLA_ROPE_DIM, 0, LANES)


def _gqa_tile_order():
    per = GQA_HEADS // GQA_KV_HEADS
    return [h for t in range(per) for h in (t, per + t)]


def _layer_weights(l, p):
    hd = GQA_HEAD_DIM
    order = _gqa_tile_order()
    qperm = np.concatenate([np.arange(hd) + hd * h for h in order])
    w_in = p["w_in"][l]
    w_in = jnp.concatenate([w_in[:, :IN_OFF[1]], w_in[:, IN_OFF[1]:IN_OFF[2]][:, qperm], w_in[:, IN_OFF[2]:]], axis=1)
    w_in = jnp.pad(w_in, ((0, 0), (0, IN_WIDTH_PAD - w_in.shape[1]))).astype(BF16)

    wq = p["mla_w_uq"][l].reshape(-1, MLA_HEADS, MLA_QK_DIM)
    rank_q = wq.shape[0]
    wq_pe = wq[:, :, MLA_NOPE_DIM:].reshape(rank_q, MLA_HEADS * MLA_ROPE_DIM)
    wq = jnp.concatenate([wq[:, :, :MLA_NOPE_DIM].reshape(rank_q, MLA_HEADS * MLA_NOPE_DIM),
                          jnp.pad(wq_pe, ((0, 0), (0, MLA_PE_LANES - wq_pe.shape[1])))], axis=1).astype(BF16)

    wkv = p["mla_w_ukv"][l].reshape(-1, MLA_HEADS, MLA_NOPE_DIM + MLA_V_DIM)
    latent = wkv.shape[0]
    wk_t = jnp.transpose(wkv[:, :, :MLA_NOPE_DIM], (1, 2, 0))
    eye = jnp.asarray(np.eye(MLA_HEADS, dtype=np.float32))
    wabs = (wk_t[:, :, None, :] * eye[:, None, :, None]).reshape(MLA_HEADS * MLA_NOPE_DIM, MLA_HEADS * latent)
    wvt = jnp.transpose(wkv[:, :, MLA_NOPE_DIM:], (1, 2, 0)).reshape(MLA_HEADS * MLA_V_DIM, latent)
    krep = np.zeros((LANES, LANES), np.float32)
    for s in range(LANES // MLA_ROPE_DIM):
        krep[np.arange(MLA_ROPE_DIM), s * MLA_ROPE_DIM + np.arange(MLA_ROPE_DIM)] = 1.0

    w_out = p["w_out"][l]
    g0 = HY_W
    g1 = g0 + GQA_HEADS * hd
    w_out = jnp.concatenate([w_out[:g0], w_out[g0:g1][qperm], w_out[g1:]], axis=0).astype(BF16)

    e2 = np.kron(np.eye(LANES // hd, dtype=np.float32), np.ones((hd, hd), np.float32))
    two = lambda g: jnp.tile(g, LANES // hd)[None, :]
    return {
        "g1": p["norm1_g"][l][None, :], "g2": p["norm2_g"][l][None, :],
        "w_in": w_in, "e2": jnp.asarray(e2, BF16),
        "gq": two(p["gqa_q_g"][l]), "gk": two(p["gqa_k_g"][l]),
        "gmq": p["mla_q_g"][l][None, :], "gmkv": p["mla_kv_g"][l][None, :],
        "wq": wq, "wabs": wabs.astype(BF16), "wvt": wvt.astype(BF16), "krep": jnp.asarray(krep, BF16),
        "w_out": w_out,
        "w1": p["ffn_w1"][l].astype(BF16), "w3": p["ffn_w3"][l].astype(BF16), "w2": p["ffn_w2"][l].astype(BF16),
        "conv_w": p["hy_conv_w"][l], "conv_b": p["hy_conv_b"][l], "hy_bias": p["hy_bias"][l],
    }


def kernel(x, c, ctx, c_ctx, mod_w, mod_b, norm1_g, norm2_g, w_in, hy_conv_w, hy_conv_b, hy_filt_w1, hy_filt_b1, hy_filt_w2, hy_filt_b2, hy_filt_w3, hy_filt_freq, hy_bias, gqa_q_g, gqa_k_g, mla_q_g, mla_kv_g, mla_w_uq, mla_w_ukv, w_out, ffn_w1, ffn_w3, ffn_w2, final_g):
    p = dict(norm1_g=norm1_g, norm2_g=norm2_g, w_in=w_in, hy_conv_w=hy_conv_w, hy_conv_b=hy_conv_b,
             hy_bias=hy_bias, gqa_q_g=gqa_q_g, gqa_k_g=gqa_k_g, mla_q_g=mla_q_g, mla_kv_g=mla_kv_g,
             mla_w_uq=mla_w_uq, mla_w_ukv=mla_w_ukv, w_out=w_out, ffn_w1=ffn_w1, ffn_w3=ffn_w3, ffn_w2=ffn_w2)
    B, L, D = x.shape
    Lc = ctx.shape[1]
    depth = mod_w.shape[0]

    pad_rows = (-(B + 1)) % 8
    c_all = jnp.concatenate([c, c_ctx[None, :], jnp.zeros((pad_rows, D), F32)], axis=0)
    mod = _modulation(c_all, mod_w, mod_b).reshape(depth, -1, N_MOD, D)

    tabs_lat = _rope_tables(L, True)
    tabs_ctx = _rope_tables(Lc, False)
    final_g2 = final_g[None, :]

    x_lat, x_ctx = x, ctx
    for l in range(depth):
        last = l == depth - 1
        lw = _layer_weights(l, p)
        filt = (hy_filt_w1[l], hy_filt_b1[l], hy_filt_w2[l], hy_filt_b2[l], hy_filt_w3[l], hy_filt_freq[l])
        mod_lat = mod[l, :B]
        mod_ctx = jnp.broadcast_to(mod[l, B][None], (B, N_MOD, D))

        hy_c, q_c, k_c, vt_c, qa_c, qp_c, km_c, vmt_c = _inproj(x_ctx, mod_ctx, lw, tabs_ctx)
        hy, q, k, vt, qa, qp, km, vmt = _inproj(x_lat, mod_lat, lw, tabs_lat)

        if not last:
            yh_c = _hyena(hy_c, lw, _hyena_filters(Lc, *filt))
            yg_c = _gqa_attention(q_c, [k_c], [vt_c])
            ym_c = _mla_attention(qa_c, qp_c, lw["wvt"], [km_c], [vmt_c])
            x_ctx_next = _outffn(x_ctx, yh_c, yg_c, ym_c, mod_ctx, lw, final_g2, False)

        yh = _hyena(hy, lw, _hyena_filters(L, *filt))
        yg = _gqa_attention(q, [k, k_c], [vt, vt_c])
        ym = _mla_attention(qa, qp, lw["wvt"], [km, km_c], [vmt, vmt_c])
        x_lat = _outffn(x_lat, yh, yg, ym, mod_lat, lw, final_g2, last)
        if not last:
            x_ctx = x_ctx_next
    return x_lat
```

```python
import functools
import math

import jax
import jax.numpy as jnp
import numpy as np
from jax import lax
from jax.experimental import pallas as pl
from jax.experimental.pallas import tpu as pltpu

F32 = jnp.float32
BF16 = jnp.bfloat16

N_MOD = 6
GRID_W = 64
NORM_EPS = 1e-6
ROPE_THETA = 10000.0
HY_W = 256
HY_ORDER = 2
HY_BANDS = 16
HY_TARGET = 1e-2
HY_FAST_PCT = 0.3
HY_SLOW_PCT = 1.5
GQA_HEADS = 6
GQA_KV_HEADS = 2
GQA_HEAD_DIM = 64
MLA_HEADS = 6
MLA_NOPE_DIM = 64
MLA_ROPE_DIM = 32
MLA_V_DIM = 64
MLA_QK_DIM = MLA_NOPE_DIM + MLA_ROPE_DIM
IN_SPLITS = (768, 384, 128, 128, 256, 128, 32)
IN_OFF = tuple(int(v) for v in np.cumsum((0,) + IN_SPLITS))
IN_WIDTH_PAD = 1920

LANES = 128
SUBLANES = 8
MXU_TILE = 256
MLA_PE_LANES = -(-MLA_HEADS * MLA_ROPE_DIM // LANES) * LANES
VMEM_LIMIT = 56 * 1024 * 1024
CONV_BLK = 256
CONV_CH = 8

LOG2E = 1.4426950408889634
HIGHEST = lax.Precision.HIGHEST


def _cparams(*sem):
    return pltpu.CompilerParams(dimension_semantics=sem, vmem_limit_bytes=VMEM_LIMIT)


def _const_spec(shape):
    zeros = (0,) * len(shape)
    return pl.BlockSpec(shape, lambda *_: zeros, pipeline_mode=pl.Buffered(1))


def _mod_kernel(c_ref, w_ref, b_ref, o_ref):
    c = c_ref[...]
    sc = c * jax.nn.sigmoid(c)
    o_ref[0] = jnp.dot(sc, w_ref[0], precision=HIGHEST, preferred_element_type=F32) + b_ref[0]


def _modulation(c_all, mod_w, mod_b):
    depth, d, n = mod_w.shape
    rows = c_all.shape[0]
    tn = 1536
    return pl.pallas_call(
        _mod_kernel,
        grid=(depth, n // tn),
        in_specs=[
            pl.BlockSpec((rows, d), lambda l, j: (0, 0)),
            pl.BlockSpec((1, d, tn), lambda l, j: (l, 0, j)),
            pl.BlockSpec((1, 1, tn), lambda l, j: (l, 0, j)),
        ],
        out_specs=pl.BlockSpec((1, rows, tn), lambda l, j: (l, 0, j)),
        out_shape=jax.ShapeDtypeStruct((depth, rows, n), F32),
        compiler_params=_cparams("parallel", "parallel"),
        name="modulation",
    )(c_all, mod_w, mod_b.reshape(depth, 1, n))


def _filter_kernel(zt_ref, w1t_ref, b1_ref, w2t_ref, b2_ref, fr_ref, w3f_ref, w3b_ref,
                   trow_ref, sel_ref, delta_ref, o_ref, h_ref):
    @pl.when(pl.program_id(0) == 0)
    def _():
        fr = fr_ref[...]
        h = jnp.sin(fr * (jnp.dot(w1t_ref[...], zt_ref[...], precision=HIGHEST,
                                  preferred_element_type=F32) + b1_ref[...]))
        h_ref[...] = jnp.sin(fr * (jnp.dot(w2t_ref[...], h, precision=HIGHEST,
                                           preferred_element_type=F32) + b2_ref[...]))

    h = h_ref[...]
    fwd = jnp.dot(w3f_ref[...], h, precision=HIGHEST, preferred_element_type=F32)
    bwd = jnp.dot(w3b_ref[...], h, precision=HIGHEST, preferred_element_type=F32)
    sel = sel_ref[...]
    k = jnp.where(sel > 0.0, fwd, jnp.where(sel < 0.0, bwd, 0.0))
    k = k * jnp.exp(-delta_ref[...] * trow_ref[...])
    nrm = jnp.sum(jnp.abs(k), axis=1, keepdims=True)
    o_ref[0] = k / nrm


def _filter_features(L):
    t = np.linspace(0.0, 1.0, L, dtype=np.float32)
    wpos = np.float32(2.0 * math.pi) * np.arange(L, dtype=np.float32) / np.float32(L)
    f = np.linspace(1e-4, HY_BANDS - 1, HY_BANDS, dtype=np.float32)
    ang = (f[None, :] * wpos[:, None]).astype(np.float64)
    z = np.concatenate([t[:, None], np.cos(ang), -np.sin(ang)], axis=1).astype(np.float32)
    n = np.arange(2 * L)
    lag = np.where(n <= L, np.minimum(n, L - 1), 2 * L - n)
    sel = np.where(n < L, 1.0, np.where(n > L, -1.0, 0.0)).astype(np.float32)
    zt = np.zeros((LANES, 2 * L), np.float32)
    zt[:z.shape[1]] = z[lag].T
    return zt, t[lag][None, :], sel[None, :]


def _hyena_filters(L, w1, b1, w2, b2, w3, freq):
    P = 2 * L
    zt, trow, sel = _filter_features(L)
    nf = w1.shape[1]
    w1t = jnp.pad(w1, ((0, LANES - w1.shape[0]), (0, 0))).T
    col = lambda a: a[:, None]
    deltas = np.abs(np.linspace(math.log(HY_TARGET) / HY_FAST_PCT, math.log(HY_TARGET) / HY_SLOW_PCT,
                                HY_W, dtype=np.float32))[:, None]
    halves = HY_W // LANES
    cb = 2 * halves
    whole = lambda shape: pl.BlockSpec(shape, lambda s: (0, 0))
    return pl.pallas_call(
        _filter_kernel,
        grid=(HY_ORDER * halves,),
        in_specs=[
            whole((LANES, P)), whole((nf, LANES)), whole((nf, 1)), whole((nf, nf)), whole((nf, 1)),
            whole((nf, 1)),
            pl.BlockSpec((LANES, nf), lambda s: ((s // halves) * cb + s % halves, 0)),
            pl.BlockSpec((LANES, nf), lambda s: ((s // halves) * cb + halves + s % halves, 0)),
            whole((1, P)), whole((1, P)),
            pl.BlockSpec((LANES, 1), lambda s: (s % halves, 0)),
        ],
        out_specs=pl.BlockSpec((1, LANES, P), lambda s: (s // halves, s % halves, 0)),
        out_shape=jax.ShapeDtypeStruct((HY_ORDER, HY_W, P), F32),
        scratch_shapes=[pltpu.VMEM((nf, P), F32)],
        compiler_params=_cparams("arbitrary"),
        name="hyena_filters",
    )(zt, w1t, col(b1), w2.T, col(b2), col(freq), w3.T, w3.T, trow, sel, deltas)


def _rope(t, c_ref, s1_ref, s2_ref, sh):
    return (t * c_ref[...] + pltpu.roll(t, LANES - sh, 1) * s1_ref[...]
            + pltpu.roll(t, sh, 1) * s2_ref[...])


def _inproj_kernel(x_ref, xp_ref, xn_ref, mod_ref, g1_ref, w_ref, cw_ref, cb_ref,
                   e2_ref, gq_ref, gk_ref, gmq_ref, gmkv_ref, wq_ref, wabs_ref, krep_ref,
                   cg_ref, s1g_ref, s2g_ref, cm_ref, s1m_ref, s2m_ref,
                   ut_ref, q_ref, k_ref, vt_ref, qabs_ref, qpe_ref, km_ref, vmt_ref, *, q_scale, qm_scale):
    shift = mod_ref[0, 0:1, :]
    scale = mod_ref[0, 1:2, :]

    def norm_mod(xx):
        hh = xx * lax.rsqrt(jnp.mean(xx * xx, axis=-1, keepdims=True) + NORM_EPS) * g1_ref[...]
        return (hh * (1.0 + scale) + shift).astype(BF16)

    u = jnp.dot(norm_mod(x_ref[0]), w_ref[...], preferred_element_type=F32)

    hy_w = IN_OFF[1]
    tm = u.shape[0]
    i, n_i = pl.program_id(0), pl.num_programs(0)
    edge = jnp.dot(norm_mod(jnp.concatenate([xp_ref[0], xn_ref[0]], axis=0)), w_ref[:, 0:hy_w],
                   preferred_element_type=F32)
    before = jnp.where(i > 0, edge[SUBLANES - 1:SUBLANES, :], 0.0)
    after = jnp.where(i < n_i - 1, edge[SUBLANES:SUBLANES + 1, :], 0.0)
    uh = u[:, 0:hy_w]
    row = lax.broadcasted_iota(jnp.int32, uh.shape, 0)
    prev = jnp.where(row == 0, before, pltpu.roll(uh, 1, 0))
    nxt = jnp.where(row == tm - 1, after, pltpu.roll(uh, tm - 1, 0))
    y = prev * cw_ref[0:1, :] + uh * cw_ref[1:2, :] + nxt * cw_ref[2:3, :] + cb_ref[...]
    for p in range(hy_w // HY_W):
        ut_ref[p, 0] = y[:, HY_W * p: HY_W * (p + 1)].T

    e2 = e2_ref[...]

    def head_norm(t, g):
        sq = t * t
        hi = sq.astype(BF16)
        lo = (sq - hi.astype(F32)).astype(BF16)
        ss = (jnp.dot(hi, e2, preferred_element_type=F32) + jnp.dot(lo, e2, preferred_element_type=F32))
        return t * lax.rsqrt(ss * (1.0 / GQA_HEAD_DIM) + NORM_EPS) * g

    for i in range(GQA_HEADS * GQA_HEAD_DIM // LANES):
        t = u[:, IN_OFF[1] + LANES * i: IN_OFF[1] + LANES * (i + 1)]
        t = _rope(head_norm(t, gq_ref[...]), cg_ref, s1g_ref, s2g_ref, GQA_HEAD_DIM // 4)
        q_ref[0, :, LANES * i: LANES * (i + 1)] = (t * q_scale).astype(BF16)
    t = u[:, IN_OFF[2]:IN_OFF[3]]
    k_ref[0] = _rope(head_norm(t, gk_ref[...]), cg_ref, s1g_ref, s2g_ref, GQA_HEAD_DIM // 4).astype(BF16)
    vt_ref[0] = u[:, IN_OFF[3]:IN_OFF[4]].T.astype(BF16)

    mq = u[:, IN_OFF[4]:IN_OFF[5]]
    mqn = mq * lax.rsqrt(jnp.mean(mq * mq, axis=-1, keepdims=True) + NORM_EPS) * gmq_ref[...]
    qm = jnp.dot(mqn.astype(BF16), wq_ref[...], preferred_element_type=F32)
    n_nope = MLA_HEADS * MLA_NOPE_DIM
    qabs = jnp.dot(qm[:, :n_nope].astype(BF16), wabs_ref[...], preferred_element_type=F32)
    qabs_ref[0] = (qabs * qm_scale).astype(BF16)
    for i in range(qpe_ref.shape[2] // LANES):
        t = _rope(qm[:, n_nope + LANES * i: n_nope + LANES * (i + 1)], cm_ref, s1m_ref, s2m_ref, MLA_ROPE_DIM // 4)
        qpe_ref[0, :, LANES * i: LANES * (i + 1)] = (t * qm_scale).astype(BF16)

    mkv = u[:, IN_OFF[5]:IN_OFF[6]]
    kvn = mkv * lax.rsqrt(jnp.mean(mkv * mkv, axis=-1, keepdims=True) + NORM_EPS) * gmkv_ref[...]
    kpe = _rope(u[:, IN_OFF[6]:IN_OFF[6] + LANES], cm_ref, s1m_ref, s2m_ref, MLA_ROPE_DIM // 4).astype(BF16)
    kpe4 = jnp.dot(kpe, krep_ref[...], preferred_element_type=F32).astype(BF16)
    km_ref[0] = jnp.concatenate([kvn.astype(BF16), kpe4], axis=1)
    vmt_ref[0] = kvn.T.astype(BF16)


def _inproj(x, mod, lw, tabs):
    B, L, D = x.shape
    tm = min(512, L)
    tab_spec = pl.BlockSpec((tm, LANES), lambda i, b: (i, 0))
    tok = lambda w: pl.BlockSpec((1, tm, w), lambda i, b: (b, i, 0))
    kern = functools.partial(_inproj_kernel,
                             q_scale=GQA_HEAD_DIM ** -0.5 * LOG2E, qm_scale=MLA_QK_DIM ** -0.5 * LOG2E)
    consts = [lw["g1"], lw["w_in"], lw["conv_w"], lw["conv_b"], lw["e2"], lw["gq"], lw["gk"], lw["gmq"], lw["gmkv"],
              lw["wq"], lw["wabs"], lw["krep"]]
    latent = lw["wabs"].shape[1] // MLA_HEADS
    out_w = [(IN_SPLITS[1], BF16, False), (IN_SPLITS[2], BF16, False),
             (IN_SPLITS[3], BF16, True), (MLA_HEADS * latent, BF16, False), (MLA_PE_LANES, BF16, False),
             (latent + LANES, BF16, False), (latent, BF16, True)]
    tok_t = lambda w: pl.BlockSpec((1, w, tm), lambda i, b: (b, 0, i))
    nparts = IN_SPLITS[0] // HY_W
    per = tm // SUBLANES
    x_before = pl.BlockSpec((1, SUBLANES, D), lambda i, b: (b, jnp.maximum(i * per - 1, 0), 0))
    x_after = pl.BlockSpec((1, SUBLANES, D), lambda i, b: (b, jnp.minimum((i + 1) * per, L // SUBLANES - 1), 0))
    return pl.pallas_call(
        kern,
        grid=(L // tm, B),
        in_specs=[tok(D), x_before, x_after, pl.BlockSpec((1, N_MOD, D), lambda i, b: (b, 0, 0))]
        + [_const_spec(a.shape) for a in consts] + [tab_spec] * len(tabs),
        out_specs=[pl.BlockSpec((nparts, 1, HY_W, tm), lambda i, b: (0, b, 0, i))]
        + [tok_t(w) if tr else tok(w) for w, _, tr in out_w],
        out_shape=[jax.ShapeDtypeStruct((nparts, B, HY_W, L), F32)]
        + [jax.ShapeDtypeStruct((B, w, L) if tr else (B, L, w), dt) for w, dt, tr in out_w],
        compiler_params=_cparams("parallel", "parallel"),
        name="inproj",
    )(x, x, x, mod, *consts, *tabs)


def _longconv_kernel(bias_ref, kf_ref, u_ref, o_ref, s0_ref, s1_ref, acc_ref, x_ref, y_ref, *, nblk):
    g = pl.program_id(0)
    B, CH = u_ref.shape[1], u_ref.shape[2]
    P = s0_ref.shape[1]
    for p in range(3):
        x_ref[p] = jnp.swapaxes(u_ref[p], 0, 1)

    def to_rows(z):
        return jnp.concatenate([z[:, j * CONV_BLK:(j + 1) * CONV_BLK] for j in range(nblk)], axis=0)

    def conv(zin, o, c, s_ref):
        krow = kf_ref[o, pl.ds(c, 1), :]
        s_ref[...] = pltpu.roll(jnp.broadcast_to(krow, (CONV_BLK, P)), 0, 1,
                                stride=1, stride_axis=0).astype(BF16)
        zb = zin.astype(BF16)
        acc_ref[...] = jnp.zeros_like(acc_ref)
        for d in range(-(nblk - 1), nblk):
            j0 = max(0, -d)
            n = nblk - abs(d)
            i0 = j0 + d
            col = (d * CONV_BLK) % P
            acc_ref[B * i0: B * (i0 + n), :] += jnp.dot(
                zb[B * j0: B * (j0 + n), :], s_ref[:, col: col + CONV_BLK],
                preferred_element_type=F32)
        return acc_ref[...] + zin * bias_ref[o, g * CH + c]

    def channel(c, carry):
        z1 = to_rows(x_ref[1, c]) * conv(to_rows(x_ref[0, c]), 0, c, s0_ref)
        y = to_rows(x_ref[2, c]) * conv(z1, 1, c, s1_ref)
        y_ref[c] = jnp.concatenate([y[B * i: B * (i + 1)] for i in range(nblk)], axis=1)
        return carry

    lax.fori_loop(0, CH, channel, 0, unroll=4)
    o_ref[...] = jnp.swapaxes(y_ref[...], 0, 1)


def _longconv(ut, kf, hy_bias):
    _, B, C, L = ut.shape
    P = kf.shape[-1]
    nblk = L // CONV_BLK
    return pl.pallas_call(
        functools.partial(_longconv_kernel, nblk=nblk),
        grid=(C // CONV_CH,),
        in_specs=[
            pl.BlockSpec(memory_space=pltpu.SMEM),
            pl.BlockSpec((HY_ORDER, CONV_CH, P), lambda g: (0, g, 0)),
            pl.BlockSpec((3, B, CONV_CH, L), lambda g: (0, 0, g, 0)),
        ],
        out_specs=pl.BlockSpec((B, CONV_CH, L), lambda g: (0, g, 0)),
        out_shape=jax.ShapeDtypeStruct((B, C, L), F32),
        scratch_shapes=[pltpu.VMEM((CONV_BLK, P), BF16), pltpu.VMEM((CONV_BLK, P), BF16),
                        pltpu.VMEM((nblk * B, CONV_BLK), F32),
                        pltpu.VMEM((3, CONV_CH, B, L), F32), pltpu.VMEM((CONV_CH, B, L), F32)],
        compiler_params=_cparams("parallel"),
        name="hyena_longconv",
    )(hy_bias, kf, ut)


ONES_ROWS = 16


def _attn_units_step(qs, k_refs, vt_refs, s_new, s_old, m_new, m_old, koffs, voffs, vrows, tk):
    nu = len(qs)
    M = qs[0].shape[0]
    dn = (((1,), (1,)), ((), ()))
    row = 0
    for k_ref in k_refs:
        n = k_ref.shape[1]
        for u in range(nu):
            s_new[u, row:row + n, :] = lax.dot_general(k_ref[0, :, koffs[u]:koffs[u] + qs[u].shape[1]], qs[u], dn,
                                                       preferred_element_type=F32)
        row += n
    mo = [m_old[u, 0:1, :] for u in range(nu)]
    mx = [jnp.full((SUBLANES, M), -jnp.inf, F32) for _ in range(nu)]
    acc = [jnp.zeros((vrows + ONES_ROWS, M), F32) for _ in range(nu)]
    ones = jnp.ones((ONES_ROWS, tk), BF16)
    row = 0
    for vt_ref in vt_refs:
        for c in range(vt_ref.shape[2] // tk):
            rows = slice(row, row + tk)
            row += tk
            for u in range(nu):
                p = jnp.exp2(s_old[u, rows, :] - mo[u]).astype(BF16)
                vt = jnp.concatenate([vt_ref[0, voffs[u]:voffs[u] + vrows, c * tk:(c + 1) * tk], ones], axis=0)
                acc[u] = acc[u] + jnp.dot(vt, p, preferred_element_type=F32)
                sn = s_new[u, rows, :]
                for r in range(tk // SUBLANES):
                    mx[u] = jnp.maximum(mx[u], sn[SUBLANES * r: SUBLANES * (r + 1), :])
    for u in range(nu):
        m_new[u] = jnp.broadcast_to(jnp.max(mx[u], axis=0, keepdims=True), (SUBLANES, M))
    return acc


def _attn_pipeline(step, s0_ref, s1_ref, m0_ref, m1_ref):
    g = pl.program_id(0)

    @pl.when(g == 0)
    def _():
        s1_ref[...] = jnp.zeros_like(s1_ref)
        m1_ref[...] = jnp.zeros_like(m1_ref)

    @pl.when(g % 2 == 0)
    def _():
        step(s0_ref, s1_ref, m0_ref, m1_ref)

    @pl.when(g % 2 == 1)
    def _():
        step(s1_ref, s0_ref, m1_ref, m0_ref)


def _gqa_kernel(*refs, nseg, tk):
    q_ref, k_refs, v_refs = refs[0], refs[1:1 + nseg], refs[1 + nseg:1 + 2 * nseg]
    o_ref, s0_ref, s1_ref, m0_ref, m1_ref = refs[1 + 2 * nseg:]
    tq = q_ref.shape[1]
    ntile = q_ref.shape[2] // LANES
    vrows = GQA_KV_HEADS * GQA_HEAD_DIM

    def step(s_new, s_old, m_new, m_old):
        low = lax.broadcasted_iota(jnp.int32, (tq, LANES), 1) < GQA_HEAD_DIM
        zero = jnp.zeros((tq, LANES), BF16)
        qs = []
        for t in range(ntile):
            qt = q_ref[0, :, LANES * t: LANES * (t + 1)]
            qs.append(jnp.where(low, qt, zero))
            qs.append(jnp.where(low, zero, qt))
        (acc,) = _attn_units_step([jnp.concatenate(qs, axis=0)], k_refs, v_refs, s_new, s_old, m_new, m_old,
                                  (0,), (0,), vrows, tk)
        o = acc[:vrows, :] * (1.0 / acc[vrows:vrows + 1, :])
        top = lax.broadcasted_iota(jnp.int32, (vrows, tq), 0) < GQA_HEAD_DIM
        for t in range(ntile):
            oa = o[:, (2 * t) * tq:(2 * t + 1) * tq]
            ob = o[:, (2 * t + 1) * tq:(2 * t + 2) * tq]
            o_ref[0, :, LANES * t: LANES * (t + 1)] = jnp.where(top, oa, ob).T.astype(BF16)

    _attn_pipeline(step, s0_ref, s1_ref, m0_ref, m1_ref)


def _gqa_attention(q, ks, vts):
    B, L, W = q.shape
    T = sum(k.shape[1] for k in ks)
    tq = min(128, L)
    nq = L // tq
    M = GQA_HEADS * tq
    last = B * nq - 1
    new = lambda g: jnp.minimum(g, last)
    old = lambda g: jnp.maximum(g - 1, 0)
    return pl.pallas_call(
        functools.partial(_gqa_kernel, nseg=len(ks), tk=256),
        grid=(B * nq + 1,),
        in_specs=[pl.BlockSpec((1, tq, W), lambda g: (new(g) // nq, new(g) % nq, 0))]
        + [pl.BlockSpec((1, k.shape[1], LANES), lambda g: (new(g) // nq, 0, 0)) for k in ks]
        + [pl.BlockSpec((1, LANES, v.shape[2]), lambda g: (old(g) // nq, 0, 0)) for v in vts],
        out_specs=pl.BlockSpec((1, tq, W), lambda g: (old(g) // nq, old(g) % nq, 0)),
        out_shape=jax.ShapeDtypeStruct((B, L, W), BF16),
        scratch_shapes=[pltpu.VMEM((1, T, M), F32), pltpu.VMEM((1, T, M), F32),
                        pltpu.VMEM((1, SUBLANES, M), F32), pltpu.VMEM((1, SUBLANES, M), F32)],
        compiler_params=_cparams("arbitrary"),
        name="gqa_attention",
    )(q, *ks, *vts)


def _mla_kernel(*refs, nseg, tk):
    qabs_ref, qpe_ref, wvt_ref = refs[0:3]
    k_refs, v_refs = refs[3:3 + nseg], refs[3 + nseg:3 + 2 * nseg]
    o_ref, s0_ref, s1_ref, m0_ref, m1_ref = refs[3 + 2 * nseg:]
    tq = qabs_ref.shape[1]
    latent = v_refs[0].shape[1]
    slots = LANES // MLA_ROPE_DIM

    def step(s_new, s_old, m_new, m_old):
        lane = lax.broadcasted_iota(jnp.int32, (tq, LANES), 1)
        zero = jnp.zeros((tq, LANES), BF16)
        qs = []
        for h in range(MLA_HEADS):
            pe = qpe_ref[0, :, LANES * (h // slots): LANES * (h // slots + 1)]
            mine = (lane >= MLA_ROPE_DIM * (h % slots)) & (lane < MLA_ROPE_DIM * (h % slots + 1))
            qs.append(jnp.concatenate([qabs_ref[0, :, latent * h: latent * (h + 1)], jnp.where(mine, pe, zero)], axis=1))
        (acc,) = _attn_units_step([jnp.concatenate(qs, axis=0)], k_refs, v_refs, s_new, s_old, m_new, m_old,
                                  (0,), (0,), latent, tk)
        olat = (acc[:latent, :] * (1.0 / acc[latent:latent + 1, :])).astype(BF16)
        outs = [jnp.dot(wvt_ref[MLA_V_DIM * h: MLA_V_DIM * (h + 1), :], olat[:, tq * h: tq * (h + 1)],
                        preferred_element_type=F32) for h in range(MLA_HEADS)]
        for t in range(MLA_HEADS // 2):
            o_ref[0, :, LANES * t: LANES * (t + 1)] = jnp.concatenate(outs[2 * t: 2 * t + 2], axis=0).T.astype(BF16)

    _attn_pipeline(step, s0_ref, s1_ref, m0_ref, m1_ref)


def _mla_attention(qabs, qpe, wvt, ks, vts):
    B, L, _ = qabs.shape
    T = sum(k.shape[1] for k in ks)
    tq = min(128, L)
    nq = L // tq
    M = MLA_HEADS * tq
    last = B * nq - 1
    new = lambda g: jnp.minimum(g, last)
    old = lambda g: jnp.maximum(g - 1, 0)
    W = MLA_HEADS * MLA_V_DIM
    return pl.pallas_call(
        functools.partial(_mla_kernel, nseg=len(ks), tk=256),
        grid=(B * nq + 1,),
        in_specs=[pl.BlockSpec((1, tq, qabs.shape[2]), lambda g: (new(g) // nq, new(g) % nq, 0)),
                  pl.BlockSpec((1, tq, qpe.shape[2]), lambda g: (new(g) // nq, new(g) % nq, 0)),
                  _const_spec(wvt.shape)]
        + [pl.BlockSpec((1,) + k.shape[1:], lambda g: (new(g) // nq, 0, 0)) for k in ks]
        + [pl.BlockSpec((1,) + v.shape[1:], lambda g: (old(g) // nq, 0, 0)) for v in vts],
        out_specs=pl.BlockSpec((1, tq, W), lambda g: (old(g) // nq, old(g) % nq, 0)),
        out_shape=jax.ShapeDtypeStruct((B, L, W), BF16),
        scratch_shapes=[pltpu.VMEM((1, T, M), F32), pltpu.VMEM((1, T, M), F32),
                        pltpu.VMEM((1, SUBLANES, M), F32), pltpu.VMEM((1, SUBLANES, M), F32)],
        compiler_params=_cparams("arbitrary"),
        name="mla_attention",
    )(qabs, qpe, wvt, *ks, *vts)


def _outffn_kernel(x_ref, yh_ref, yg_ref, ym_ref, mod_ref, g2_ref, wo_ref, w1_ref, w3_ref, w2_ref,
                   gf_ref, o_ref, *, hchunk, final):
    x = x_ref[0]
    mix = jnp.concatenate([yh_ref[0].T.astype(BF16), yg_ref[0], ym_ref[0]], axis=1)
    y = jnp.dot(mix, wo_ref[...], preferred_element_type=F32)
    x1 = x + mod_ref[0, 2:3, :] * y
    h = x1 * lax.rsqrt(jnp.mean(x1 * x1, axis=-1, keepdims=True) + NORM_EPS) * g2_ref[...]
    h = (h * (1.0 + mod_ref[0, 4:5, :]) + mod_ref[0, 3:4, :]).astype(BF16)
    f = jnp.zeros_like(x)
    hidden = w1_ref.shape[1]
    for lo in range(0, hidden, hchunk):
        cols = slice(lo, min(lo + hchunk, hidden))
        a = jnp.dot(h, w1_ref[:, cols], preferred_element_type=F32)
        b = jnp.dot(h, w3_ref[:, cols], preferred_element_type=F32)
        g = (a * jax.nn.sigmoid(a) * b).astype(BF16)
        f = f + jnp.dot(g, w2_ref[cols, :], preferred_element_type=F32)
    x2 = x1 + mod_ref[0, 5:6, :] * f
    if final:
        x2 = x2 * lax.rsqrt(jnp.mean(x2 * x2, axis=-1, keepdims=True) + NORM_EPS) * gf_ref[...]
    o_ref[0] = x2


def _outffn(x, yh_t, yg, ym, mod, lw, final_g, final):
    B, L, D = x.shape
    tm = min(512, L)
    hidden = lw["w1"].shape[1]
    consts = [lw["g2"], lw["w_out"], lw["w1"], lw["w3"], lw["w2"], final_g]
    kern = functools.partial(_outffn_kernel, hchunk=6 * MXU_TILE, final=final)
    tok = lambda w: pl.BlockSpec((1, tm, w), lambda b, i: (b, i, 0))
    return pl.pallas_call(
        kern,
        grid=(B, L // tm),
        in_specs=[tok(D), pl.BlockSpec((1, HY_W, tm), lambda b, i: (b, 0, i)),
                  tok(yg.shape[2]), tok(ym.shape[2]),
                  pl.BlockSpec((1, N_MOD, D), lambda b, i: (b, 0, 0))]
        + [_const_spec(a.shape) for a in consts],
        out_specs=tok(D),
        out_shape=jax.ShapeDtypeStruct((B, L, D), F32),
        compiler_params=_cparams("parallel", "parallel"),
        name="outffn",
    )(x, yh_t, yg, ym, mod, *consts)


def _rope_tables(L, with_pos):
    lane = np.arange(LANES)

    def pattern(dim, off, width):
        loc = (lane - off) % dim
        half = dim // 2
        active = (lane >= off) & (lane < off + width)
        use_col = loc >= half
        fi = (loc % half) % (half // 2)
        first = (loc % half) < (half // 2)
        return active, use_col, fi, first, half

    def tables(dim, off, width):
        active, use_col, fi, first, half = pattern(dim, off, width)
        if not with_pos:
            one = jnp.ones((L, LANES), F32)
            zero = jnp.zeros((L, LANES), F32)
            return [one, zero, zero]
        row = jnp.repeat(jnp.arange(L // GRID_W, dtype=jnp.int32), GRID_W).astype(F32)
        col = jnp.tile(jnp.arange(GRID_W, dtype=jnp.int32), L // GRID_W).astype(F32)
        inv = ROPE_THETA ** (-jnp.arange(0, half, 2, dtype=F32) / half)
        pos = jnp.where(jnp.asarray(use_col)[None, :], col[:, None], row[:, None])
        ang = pos * inv[jnp.asarray(fi)][None, :]
        act = jnp.asarray(active)[None, :]
        fst = jnp.asarray(first)[None, :]
        cos = jnp.where(act, jnp.cos(ang), 1.0)
        sin = jnp.sin(ang)
        s1 = jnp.where(act & fst, -sin, 0.0)
        s2 = jnp.where(act & ~fst, sin, 0.0)
        return [cos, s1, s2]

    return tables(GQA_HEAD_DIM, 0, LANES) + tables(MLA_ROPE_DIM, 0, LANES)


def _gqa_tile_order():
    per = GQA_HEADS // GQA_KV_HEADS
    return [h for t in range(per) for h in (t, per + t)]


def _layer_weights(l, p):
    hd = GQA_HEAD_DIM
    order = _gqa_tile_order()
    qperm = np.concatenate([np.arange(hd) + hd * h for h in order])
    w_in = p["w_in"][l]
    w_in = jnp.concatenate([w_in[:, :IN_OFF[1]], w_in[:, IN_OFF[1]:IN_OFF[2]][:, qperm], w_in[:, IN_OFF[2]:]], axis=1)
    w_in = jnp.pad(w_in, ((0, 0), (0, IN_WIDTH_PAD - w_in.shape[1]))).astype(BF16)

    wq = p["mla_w_uq"][l].reshape(-1, MLA_HEADS, MLA_QK_DIM)
    rank_q = wq.shape[0]
    wq_pe = wq[:, :, MLA_NOPE_DIM:].reshape(rank_q, MLA_HEADS * MLA_ROPE_DIM)
    wq = jnp.concatenate([wq[:, :, :MLA_NOPE_DIM].reshape(rank_q, MLA_HEADS * MLA_NOPE_DIM),
                          jnp.pad(wq_pe, ((0, 0), (0, MLA_PE_LANES - wq_pe.shape[1])))], axis=1).astype(BF16)

    wkv = p["mla_w_ukv"][l].reshape(-1, MLA_HEADS, MLA_NOPE_DIM + MLA_V_DIM)
    latent = wkv.shape[0]
    wk_t = jnp.transpose(wkv[:, :, :MLA_NOPE_DIM], (1, 2, 0))
    eye = jnp.asarray(np.eye(MLA_HEADS, dtype=np.float32))
    wabs = (wk_t[:, :, None, :] * eye[:, None, :, None]).reshape(MLA_HEADS * MLA_NOPE_DIM, MLA_HEADS * latent)
    wvt = jnp.transpose(wkv[:, :, MLA_NOPE_DIM:], (1, 2, 0)).reshape(MLA_HEADS * MLA_V_DIM, latent)
    krep = np.zeros((LANES, LANES), np.float32)
    for s in range(LANES // MLA_ROPE_DIM):
        krep[np.arange(MLA_ROPE_DIM), s * MLA_ROPE_DIM + np.arange(MLA_ROPE_DIM)] = 1.0

    w_out = p["w_out"][l]
    g0 = HY_W
    g1 = g0 + GQA_HEADS * hd
    w_out = jnp.concatenate([w_out[:g0], w_out[g0:g1][qperm], w_out[g1:]], axis=0).astype(BF16)

    e2 = np.kron(np.eye(LANES // hd, dtype=np.float32), np.ones((hd, hd), np.float32))
    two = lambda g: jnp.tile(g, LANES // hd)[None, :]
    return {
        "g1": p["norm1_g"][l][None, :], "g2": p["norm2_g"][l][None, :],
        "w_in": w_in, "e2": jnp.asarray(e2, BF16),
        "gq": two(p["gqa_q_g"][l]), "gk": two(p["gqa_k_g"][l]),
        "gmq": p["mla_q_g"][l][None, :], "gmkv": p["mla_kv_g"][l][None, :],
        "wq": wq, "wabs": wabs.astype(BF16), "wvt": wvt.astype(BF16), "krep": jnp.asarray(krep, BF16),
        "w_out": w_out,
        "w1": p["ffn_w1"][l].astype(BF16), "w3": p["ffn_w3"][l].astype(BF16), "w2": p["ffn_w2"][l].astype(BF16),
        "conv_w": p["hy_conv_w"][l], "conv_b": p["hy_conv_b"][l][None, :], "hy_bias": p["hy_bias"][l],
    }


def kernel(x, c, ctx, c_ctx, mod_w, mod_b, norm1_g, norm2_g, w_in, hy_conv_w, hy_conv_b, hy_filt_w1, hy_filt_b1, hy_filt_w2, hy_filt_b2, hy_filt_w3, hy_filt_freq, hy_bias, gqa_q_g, gqa_k_g, mla_q_g, mla_kv_g, mla_w_uq, mla_w_ukv, w_out, ffn_w1, ffn_w3, ffn_w2, final_g):
    p = dict(norm1_g=norm1_g, norm2_g=norm2_g, w_in=w_in, hy_conv_w=hy_conv_w, hy_conv_b=hy_conv_b,
             hy_bias=hy_bias, gqa_q_g=gqa_q_g, gqa_k_g=gqa_k_g, mla_q_g=mla_q_g, mla_kv_g=mla_kv_g,
             mla_w_uq=mla_w_uq, mla_w_ukv=mla_w_ukv, w_out=w_out, ffn_w1=ffn_w1, ffn_w3=ffn_w3, ffn_w2=ffn_w2)
    B, L, D = x.shape
    Lc = ctx.shape[1]
    depth = mod_w.shape[0]

    pad_rows = (-(B + 1)) % 8
    c_all = jnp.concatenate([c, c_ctx[None, :], jnp.zeros((pad_rows, D), F32)], axis=0)
    mod = _modulation(c_all, mod_w, mod_b).reshape(depth, -1, N_MOD, D)

    tabs_lat = _rope_tables(L, True)
    tabs_ctx = _rope_tables(Lc, False)
    final_g2 = final_g[None, :]

    x_lat, x_ctx = x, ctx
    for l in range(depth):
        last = l == depth - 1
        lw = _layer_weights(l, p)
        filt = (hy_filt_w1[l], hy_filt_b1[l], hy_filt_w2[l], hy_filt_b2[l], hy_filt_w3[l], hy_filt_freq[l])
        mod_lat = mod[l, :B]
        mod_ctx = jnp.broadcast_to(mod[l, B][None], (B, N_MOD, D))

        ut_c, q_c, k_c, vt_c, qa_c, qp_c, km_c, vmt_c = _inproj(x_ctx, mod_ctx, lw, tabs_ctx)
        ut, q, k, vt, qa, qp, km, vmt = _inproj(x_lat, mod_lat, lw, tabs_lat)

        if not last:
            yh_c = _longconv(ut_c, _hyena_filters(Lc, *filt), lw["hy_bias"])
            yg_c = _gqa_attention(q_c, [k_c], [vt_c])
            ym_c = _mla_attention(qa_c, qp_c, lw["wvt"], [km_c], [vmt_c])
            x_ctx_next = _outffn(x_ctx, yh_c, yg_c, ym_c, mod_ctx, lw, final_g2, False)

        yh = _longconv(ut, _hyena_filters(L, *filt), lw["hy_bias"])
        yg = _gqa_attention(q, [k, k_c], [vt, vt_c])
        ym = _mla_attention(qa, qp, lw["wvt"], [km, km_c], [vmt, vmt_c])
        x_lat = _outffn(x_lat, yh, yg, ym, mod_lat, lw, final_g2, last)
        if not last:
            x_ctx = x_ctx_next
    return x_lat
```

```python
import functools
import math

import jax
import jax.numpy as jnp
import numpy as np
from jax import lax
from jax.experimental import pallas as pl
from jax.experimental.pallas import tpu as pltpu

F32 = jnp.float32
BF16 = jnp.bfloat16

N_MOD = 6
GRID_W = 64
NORM_EPS = 1e-6
ROPE_THETA = 10000.0
HY_W = 256
HY_ORDER = 2
HY_BANDS = 16
HY_TARGET = 1e-2
HY_FAST_PCT = 0.3
HY_SLOW_PCT = 1.5
GQA_HEADS = 6
GQA_KV_HEADS = 2
GQA_HEAD_DIM = 64
MLA_HEADS = 6
MLA_NOPE_DIM = 64
MLA_ROPE_DIM = 32
MLA_V_DIM = 64
MLA_QK_DIM = MLA_NOPE_DIM + MLA_ROPE_DIM
IN_SPLITS = (768, 384, 128, 128, 256, 128, 32)
IN_OFF = tuple(int(v) for v in np.cumsum((0,) + IN_SPLITS))
IN_WIDTH_PAD = 1920

LANES = 128
SUBLANES = 8
BF16_ROWS = 16
MXU_TILE = 256
MLA_PE_LANES = -(-MLA_HEADS * MLA_ROPE_DIM // LANES) * LANES
VMEM_LIMIT = 56 * 1024 * 1024
CONV_BLK = 256
CONV_CH = 8

LOG2E = 1.4426950408889634
HIGHEST = lax.Precision.HIGHEST


def _cparams(*sem):
    return pltpu.CompilerParams(dimension_semantics=sem, vmem_limit_bytes=VMEM_LIMIT)


def _const_spec(shape):
    zeros = (0,) * len(shape)
    return pl.BlockSpec(shape, lambda *_: zeros, pipeline_mode=pl.Buffered(1))


def _mod_kernel(c_ref, w_ref, b_ref, o_ref):
    c = c_ref[...]
    sc = c * jax.nn.sigmoid(c)
    o_ref[0] = jnp.dot(sc, w_ref[0], precision=HIGHEST, preferred_element_type=F32) + b_ref[0]


def _modulation(c_all, mod_w, mod_b):
    depth, d, n = mod_w.shape
    rows = c_all.shape[0]
    tn = 1536
    return pl.pallas_call(
        _mod_kernel,
        grid=(depth, n // tn),
        in_specs=[
            pl.BlockSpec((rows, d), lambda l, j: (0, 0)),
            pl.BlockSpec((1, d, tn), lambda l, j: (l, 0, j)),
            pl.BlockSpec((1, 1, tn), lambda l, j: (l, 0, j)),
        ],
        out_specs=pl.BlockSpec((1, rows, tn), lambda l, j: (l, 0, j)),
        out_shape=jax.ShapeDtypeStruct((depth, rows, n), F32),
        compiler_params=_cparams("parallel", "parallel"),
        name="modulation",
    )(c_all, mod_w, mod_b.reshape(depth, 1, n))


def _filter_kernel(zt_ref, w1t_ref, b1_ref, w2t_ref, b2_ref, fr_ref, w3f_ref, w3b_ref,
                   trow_ref, sel_ref, delta_ref, o_ref, h_ref):
    @pl.when(pl.program_id(0) == 0)
    def _():
        fr = fr_ref[...]
        h = jnp.sin(fr * (jnp.dot(w1t_ref[...], zt_ref[...], precision=HIGHEST,
                                  preferred_element_type=F32) + b1_ref[...]))
        h_ref[...] = jnp.sin(fr * (jnp.dot(w2t_ref[...], h, precision=HIGHEST,
                                           preferred_element_type=F32) + b2_ref[...]))

    h = h_ref[...]
    fwd = jnp.dot(w3f_ref[...], h, precision=HIGHEST, preferred_element_type=F32)
    bwd = jnp.dot(w3b_ref[...], h, precision=HIGHEST, preferred_element_type=F32)
    sel = sel_ref[...]
    k = jnp.where(sel > 0.0, fwd, jnp.where(sel < 0.0, bwd, 0.0))
    k = k * jnp.exp(-delta_ref[...] * trow_ref[...])
    nrm = jnp.sum(jnp.abs(k), axis=1, keepdims=True)
    o_ref[0] = k / nrm


def _filter_features(L):
    t = np.linspace(0.0, 1.0, L, dtype=np.float32)
    wpos = np.float32(2.0 * math.pi) * np.arange(L, dtype=np.float32) / np.float32(L)
    f = np.linspace(1e-4, HY_BANDS - 1, HY_BANDS, dtype=np.float32)
    ang = (f[None, :] * wpos[:, None]).astype(np.float64)
    z = np.concatenate([t[:, None], np.cos(ang), -np.sin(ang)], axis=1).astype(np.float32)
    n = np.arange(2 * L)
    lag = np.where(n <= L, np.minimum(n, L - 1), 2 * L - n)
    sel = np.where(n < L, 1.0, np.where(n > L, -1.0, 0.0)).astype(np.float32)
    zt = np.zeros((LANES, 2 * L), np.float32)
    zt[:z.shape[1]] = z[lag].T
    return zt, t[lag][None, :], sel[None, :]


def _hyena_filters(L, w1, b1, w2, b2, w3, freq):
    P = 2 * L
    zt, trow, sel = _filter_features(L)
    nf = w1.shape[1]
    w1t = jnp.pad(w1, ((0, LANES - w1.shape[0]), (0, 0))).T
    col = lambda a: a[:, None]
    deltas = np.abs(np.linspace(math.log(HY_TARGET) / HY_FAST_PCT, math.log(HY_TARGET) / HY_SLOW_PCT,
                                HY_W, dtype=np.float32))[:, None]
    halves = HY_W // LANES
    cb = 2 * halves
    whole = lambda shape: pl.BlockSpec(shape, lambda s: (0, 0))
    return pl.pallas_call(
        _filter_kernel,
        grid=(HY_ORDER * halves,),
        in_specs=[
            whole((LANES, P)), whole((nf, LANES)), whole((nf, 1)), whole((nf, nf)), whole((nf, 1)),
            whole((nf, 1)),
            pl.BlockSpec((LANES, nf), lambda s: ((s // halves) * cb + s % halves, 0)),
            pl.BlockSpec((LANES, nf), lambda s: ((s // halves) * cb + halves + s % halves, 0)),
            whole((1, P)), whole((1, P)),
            pl.BlockSpec((LANES, 1), lambda s: (s % halves, 0)),
        ],
        out_specs=pl.BlockSpec((1, LANES, P), lambda s: (s // halves, s % halves, 0)),
        out_shape=jax.ShapeDtypeStruct((HY_ORDER, HY_W, P), F32),
        scratch_shapes=[pltpu.VMEM((nf, P), F32)],
        compiler_params=_cparams("arbitrary"),
        name="hyena_filters",
    )(zt, w1t, col(b1), w2.T, col(b2), col(freq), w3.T, w3.T, trow, sel, deltas)


def _rope(t, c_ref, s1_ref, s2_ref, sh):
    return (t * c_ref[...] + pltpu.roll(t, LANES - sh, 1) * s1_ref[...]
            + pltpu.roll(t, sh, 1) * s2_ref[...])


def _inproj_kernel(x_ref, xp_ref, xn_ref, mod_ref, g1_ref, w_ref, cw_ref, cb_ref,
                   e2_ref, gq_ref, gk_ref, gmq_ref, gmkv_ref, wq_ref, wabs_ref, krep_ref,
                   cg_ref, s1g_ref, s2g_ref, cm_ref, s1m_ref, s2m_ref,
                   ut_ref, q_ref, k_ref, vt_ref, qabs_ref, qpe_ref, km_ref, vmt_ref, *, q_scale, qm_scale):
    shift = mod_ref[0, 0:1, :]
    scale = mod_ref[0, 1:2, :]

    def norm_mod(xx):
        hh = xx * lax.rsqrt(jnp.mean(xx * xx, axis=-1, keepdims=True) + NORM_EPS) * g1_ref[...]
        return (hh * (1.0 + scale) + shift).astype(BF16)

    u = jnp.dot(norm_mod(x_ref[0]), w_ref[...], preferred_element_type=F32)

    hy_w = IN_OFF[1]
    tm = u.shape[0]
    i, n_i = pl.program_id(0), pl.num_programs(0)
    edge = jnp.dot(norm_mod(jnp.concatenate([xp_ref[0], xn_ref[0]], axis=0)), w_ref[:, 0:hy_w],
                   preferred_element_type=F32)
    before = jnp.where(i > 0, edge[SUBLANES - 1:SUBLANES, :], 0.0)
    after = jnp.where(i < n_i - 1, edge[SUBLANES:SUBLANES + 1, :], 0.0)
    uh = u[:, 0:hy_w]
    row = lax.broadcasted_iota(jnp.int32, uh.shape, 0)
    prev = jnp.where(row == 0, before, pltpu.roll(uh, 1, 0))
    nxt = jnp.where(row == tm - 1, after, pltpu.roll(uh, tm - 1, 0))
    y = prev * cw_ref[0:1, :] + uh * cw_ref[1:2, :] + nxt * cw_ref[2:3, :] + cb_ref[...]
    for p in range(hy_w // HY_W):
        ut_ref[p, 0] = y[:, HY_W * p: HY_W * (p + 1)].T

    e2 = e2_ref[...]

    def head_norm(t, g):
        sq = t * t
        hi = sq.astype(BF16)
        lo = (sq - hi.astype(F32)).astype(BF16)
        ss = (jnp.dot(hi, e2, preferred_element_type=F32) + jnp.dot(lo, e2, preferred_element_type=F32))
        return t * lax.rsqrt(ss * (1.0 / GQA_HEAD_DIM) + NORM_EPS) * g

    for i in range(GQA_HEADS * GQA_HEAD_DIM // LANES):
        t = u[:, IN_OFF[1] + LANES * i: IN_OFF[1] + LANES * (i + 1)]
        t = _rope(head_norm(t, gq_ref[...]), cg_ref, s1g_ref, s2g_ref, GQA_HEAD_DIM // 4)
        q_ref[0, :, LANES * i: LANES * (i + 1)] = (t * q_scale).astype(BF16)
    t = u[:, IN_OFF[2]:IN_OFF[3]]
    k_ref[0] = _rope(head_norm(t, gk_ref[...]), cg_ref, s1g_ref, s2g_ref, GQA_HEAD_DIM // 4).astype(BF16)
    vt_ref[0] = u[:, IN_OFF[3]:IN_OFF[4]].T.astype(BF16)

    mq = u[:, IN_OFF[4]:IN_OFF[5]]
    mqn = mq * lax.rsqrt(jnp.mean(mq * mq, axis=-1, keepdims=True) + NORM_EPS) * gmq_ref[...]
    qm = jnp.dot(mqn.astype(BF16), wq_ref[...], preferred_element_type=F32)
    n_nope = MLA_HEADS * MLA_NOPE_DIM
    qabs = jnp.dot(qm[:, :n_nope].astype(BF16), wabs_ref[...], preferred_element_type=F32)
    qabs_ref[0] = (qabs * qm_scale).astype(BF16)
    for i in range(qpe_ref.shape[2] // LANES):
        t = _rope(qm[:, n_nope + LANES * i: n_nope + LANES * (i + 1)], cm_ref, s1m_ref, s2m_ref, MLA_ROPE_DIM // 4)
        qpe_ref[0, :, LANES * i: LANES * (i + 1)] = (t * qm_scale).astype(BF16)

    mkv = u[:, IN_OFF[5]:IN_OFF[6]]
    kvn = mkv * lax.rsqrt(jnp.mean(mkv * mkv, axis=-1, keepdims=True) + NORM_EPS) * gmkv_ref[...]
    kpe = _rope(u[:, IN_OFF[6]:IN_OFF[6] + LANES], cm_ref, s1m_ref, s2m_ref, MLA_ROPE_DIM // 4).astype(BF16)
    kpe4 = jnp.dot(kpe, krep_ref[...], preferred_element_type=F32).astype(BF16)
    km_ref[0] = jnp.concatenate([kvn.astype(BF16), kpe4], axis=1)
    vmt_ref[0] = kvn.T.astype(BF16)


def _inproj(x, mod, lw, tabs):
    B, L, D = x.shape
    tm = min(1024, L)
    tab_spec = pl.BlockSpec((tm, LANES), lambda i, b: (i, 0))
    tok = lambda w: pl.BlockSpec((1, tm, w), lambda i, b: (b, i, 0))
    kern = functools.partial(_inproj_kernel,
                             q_scale=GQA_HEAD_DIM ** -0.5 * LOG2E, qm_scale=MLA_QK_DIM ** -0.5 * LOG2E)
    consts = [lw["g1"], lw["w_in"], lw["conv_w"], lw["conv_b"], lw["e2"], lw["gq"], lw["gk"], lw["gmq"], lw["gmkv"],
              lw["wq"], lw["wabs"], lw["krep"]]
    latent = lw["wabs"].shape[1] // MLA_HEADS
    out_w = [(IN_SPLITS[1], BF16, False), (IN_SPLITS[2], BF16, False),
             (IN_SPLITS[3], BF16, True), (MLA_HEADS * latent, BF16, False), (MLA_PE_LANES, BF16, False),
             (latent + LANES, BF16, False), (latent, BF16, True)]
    tok_t = lambda w: pl.BlockSpec((1, w, tm), lambda i, b: (b, 0, i))
    nparts = IN_SPLITS[0] // HY_W
    per = tm // SUBLANES
    x_before = pl.BlockSpec((1, SUBLANES, D), lambda i, b: (b, jnp.maximum(i * per - 1, 0), 0))
    x_after = pl.BlockSpec((1, SUBLANES, D), lambda i, b: (b, jnp.minimum((i + 1) * per, L // SUBLANES - 1), 0))
    return pl.pallas_call(
        kern,
        grid=(L // tm, B),
        in_specs=[tok(D), x_before, x_after, pl.BlockSpec((1, N_MOD, D), lambda i, b: (b, 0, 0))]
        + [_const_spec(a.shape) for a in consts] + [tab_spec] * len(tabs),
        out_specs=[pl.BlockSpec((nparts, 1, HY_W, tm), lambda i, b: (0, b, 0, i))]
        + [tok_t(w) if tr else tok(w) for w, _, tr in out_w],
        out_shape=[jax.ShapeDtypeStruct((nparts, B, HY_W, L), F32)]
        + [jax.ShapeDtypeStruct((B, w, L) if tr else (B, L, w), dt) for w, dt, tr in out_w],
        compiler_params=_cparams("parallel", "parallel"),
        name="inproj",
    )(x, x, x, mod, *consts, *tabs)


def _longconv_kernel(bias_ref, kf_ref, u_ref, o_ref, s0_ref, s1_ref, acc_ref, x_ref, y_ref, *, nblk):
    g = pl.program_id(0)
    B, CH = u_ref.shape[1], u_ref.shape[2]
    P = s0_ref.shape[1]
    for p in range(3):
        x_ref[p] = jnp.swapaxes(u_ref[p], 0, 1)

    def to_rows(z):
        return jnp.concatenate([z[:, j * CONV_BLK:(j + 1) * CONV_BLK] for j in range(nblk)], axis=0)

    def conv(zin, o, c, s_ref):
        krow = kf_ref[o, pl.ds(c, 1), :]
        s_ref[...] = pltpu.roll(jnp.broadcast_to(krow, (CONV_BLK, P)), 0, 1,
                                stride=1, stride_axis=0).astype(BF16)
        zb = zin.astype(BF16)
        acc_ref[...] = jnp.zeros_like(acc_ref)
        for d in range(-(nblk - 1), nblk):
            j0 = max(0, -d)
            n = nblk - abs(d)
            i0 = j0 + d
            col = (d * CONV_BLK) % P
            acc_ref[B * i0: B * (i0 + n), :] += jnp.dot(
                zb[B * j0: B * (j0 + n), :], s_ref[:, col: col + CONV_BLK],
                preferred_element_type=F32)
        return acc_ref[...] + zin * bias_ref[o, g * CH + c]

    def channel(c, carry):
        z1 = to_rows(x_ref[1, c]) * conv(to_rows(x_ref[0, c]), 0, c, s0_ref)
        y = to_rows(x_ref[2, c]) * conv(z1, 1, c, s1_ref)
        y_ref[c] = jnp.concatenate([y[B * i: B * (i + 1)] for i in range(nblk)], axis=1)
        return carry

    lax.fori_loop(0, CH, channel, 0, unroll=True)
    o_ref[...] = jnp.swapaxes(y_ref[...], 0, 1)


def _longconv(ut, kf, hy_bias):
    _, B, C, L = ut.shape
    P = kf.shape[-1]
    nblk = L // CONV_BLK
    return pl.pallas_call(
        functools.partial(_longconv_kernel, nblk=nblk),
        grid=(C // CONV_CH,),
        in_specs=[
            pl.BlockSpec(memory_space=pltpu.SMEM),
            pl.BlockSpec((HY_ORDER, CONV_CH, P), lambda g: (0, g, 0)),
            pl.BlockSpec((3, B, CONV_CH, L), lambda g: (0, 0, g, 0)),
        ],
        out_specs=pl.BlockSpec((B, CONV_CH, L), lambda g: (0, g, 0)),
        out_shape=jax.ShapeDtypeStruct((B, C, L), F32),
        scratch_shapes=[pltpu.VMEM((CONV_BLK, P), BF16), pltpu.VMEM((CONV_BLK, P), BF16),
                        pltpu.VMEM((nblk * B, CONV_BLK), F32),
                        pltpu.VMEM((3, CONV_CH, B, L), F32), pltpu.VMEM((CONV_CH, B, L), F32)],
        compiler_params=_cparams("parallel"),
        name="hyena_longconv",
    )(hy_bias, kf, ut)


def _attn_units_step(qs, k_refs, vt_refs, s_new, s_old, m_new, m_old, koffs, voffs, vrows, tk):
    nu = len(qs)
    M = qs[0].shape[0]
    dn = (((1,), (1,)), ((), ()))
    row = 0
    for k_ref in k_refs:
        n = k_ref.shape[1]
        for u in range(nu):
            s_new[u, row:row + n, :] = lax.dot_general(k_ref[0, :, koffs[u]:koffs[u] + qs[u].shape[1]], qs[u], dn,
                                                       preferred_element_type=F32)
        row += n
    mo = [m_old[u, 0:1, :] for u in range(nu)]
    mx = [jnp.full((SUBLANES, M), -jnp.inf, F32) for _ in range(nu)]
    ones = jnp.ones((BF16_ROWS, tk), BF16)
    acc = [jnp.zeros((vrows + BF16_ROWS, M), F32) for _ in range(nu)]
    row = 0
    for vt_ref in vt_refs:
        for c in range(vt_ref.shape[2] // tk):
            rows = slice(row, row + tk)
            row += tk
            for u in range(nu):
                p = jnp.exp2(s_old[u, rows, :] - mo[u]).astype(BF16)
                vt = jnp.concatenate([vt_ref[0, voffs[u]:voffs[u] + vrows, c * tk:(c + 1) * tk], ones], axis=0)
                acc[u] = acc[u] + jnp.dot(vt, p, preferred_element_type=F32)
                sn = s_new[u, rows, :]
                for r in range(tk // SUBLANES):
                    mx[u] = jnp.maximum(mx[u], sn[SUBLANES * r: SUBLANES * (r + 1), :])
    for u in range(nu):
        m_new[u] = jnp.broadcast_to(jnp.max(mx[u], axis=0, keepdims=True), (SUBLANES, M))
    return [a[:vrows, :] for a in acc], [a[vrows:vrows + 1, :] for a in acc]


def _attn_pipeline(step, s0_ref, s1_ref, m0_ref, m1_ref):
    g = pl.program_id(0)

    @pl.when(g == 0)
    def _():
        s1_ref[...] = jnp.zeros_like(s1_ref)
        m1_ref[...] = jnp.zeros_like(m1_ref)

    @pl.when(g % 2 == 0)
    def _():
        step(s0_ref, s1_ref, m0_ref, m1_ref)

    @pl.when(g % 2 == 1)
    def _():
        step(s1_ref, s0_ref, m1_ref, m0_ref)


def _gqa_kernel(*refs, nseg, tk):
    q_ref, k_refs, v_refs = refs[0], refs[1:1 + nseg], refs[1 + nseg:1 + 2 * nseg]
    o_ref, s0_ref, s1_ref, m0_ref, m1_ref = refs[1 + 2 * nseg:]
    tq = q_ref.shape[1]
    ntile = q_ref.shape[2] // LANES
    vrows = GQA_KV_HEADS * GQA_HEAD_DIM

    def step(s_new, s_old, m_new, m_old):
        low = lax.broadcasted_iota(jnp.int32, (tq, LANES), 1) < GQA_HEAD_DIM
        zero = jnp.zeros((tq, LANES), BF16)
        qs = []
        for t in range(ntile):
            qt = q_ref[0, :, LANES * t: LANES * (t + 1)]
            qs.append(jnp.where(low, qt, zero))
            qs.append(jnp.where(low, zero, qt))
        (acc,), (den,) = _attn_units_step([jnp.concatenate(qs, axis=0)], k_refs, v_refs, s_new, s_old,
                                          m_new, m_old, (0,), (0,), vrows, tk)
        o = acc * (1.0 / den)
        top = lax.broadcasted_iota(jnp.int32, (vrows, tq), 0) < GQA_HEAD_DIM
        for t in range(ntile):
            oa = o[:, (2 * t) * tq:(2 * t + 1) * tq]
            ob = o[:, (2 * t + 1) * tq:(2 * t + 2) * tq]
            o_ref[0, :, LANES * t: LANES * (t + 1)] = jnp.where(top, oa, ob).T.astype(BF16)

    _attn_pipeline(step, s0_ref, s1_ref, m0_ref, m1_ref)


def _gqa_attention(q, ks, vts):
    B, L, W = q.shape
    T = sum(k.shape[1] for k in ks)
    tq = min(128, L)
    nq = L // tq
    M = GQA_HEADS * tq
    last = B * nq - 1
    new = lambda g: jnp.minimum(g, last)
    old = lambda g: jnp.maximum(g - 1, 0)
    return pl.pallas_call(
        functools.partial(_gqa_kernel, nseg=len(ks), tk=256),
        grid=(B * nq + 1,),
        in_specs=[pl.BlockSpec((1, tq, W), lambda g: (new(g) // nq, new(g) % nq, 0))]
        + [pl.BlockSpec((1, k.shape[1], LANES), lambda g: (new(g) // nq, 0, 0)) for k in ks]
        + [pl.BlockSpec((1, LANES, v.shape[2]), lambda g: (old(g) // nq, 0, 0)) for v in vts],
        out_specs=pl.BlockSpec((1, tq, W), lambda g: (old(g) // nq, old(g) % nq, 0)),
        out_shape=jax.ShapeDtypeStruct((B, L, W), BF16),
        scratch_shapes=[pltpu.VMEM((1, T, M), F32), pltpu.VMEM((1, T, M), F32),
                        pltpu.VMEM((1, SUBLANES, M), F32), pltpu.VMEM((1, SUBLANES, M), F32)],
        compiler_params=_cparams("arbitrary"),
        name="gqa_attention",
    )(q, *ks, *vts)


def _mla_kernel(*refs, nseg, tk):
    qabs_ref, qpe_ref, wvt_ref = refs[0:3]
    k_refs, v_refs = refs[3:3 + nseg], refs[3 + nseg:3 + 2 * nseg]
    o_ref, s0_ref, s1_ref, m0_ref, m1_ref = refs[3 + 2 * nseg:]
    tq = qabs_ref.shape[1]
    latent = v_refs[0].shape[1]
    slots = LANES // MLA_ROPE_DIM

    def step(s_new, s_old, m_new, m_old):
        lane = lax.broadcasted_iota(jnp.int32, (tq, LANES), 1)
        zero = jnp.zeros((tq, LANES), BF16)
        qs = []
        for h in range(MLA_HEADS):
            pe = qpe_ref[0, :, LANES * (h // slots): LANES * (h // slots + 1)]
            mine = (lane >= MLA_ROPE_DIM * (h % slots)) & (lane < MLA_ROPE_DIM * (h % slots + 1))
            qs.append(jnp.concatenate([qabs_ref[0, :, latent * h: latent * (h + 1)], jnp.where(mine, pe, zero)], axis=1))
        (acc,), (den,) = _attn_units_step([jnp.concatenate(qs, axis=0)], k_refs, v_refs, s_new, s_old,
                                          m_new, m_old, (0,), (0,), latent, tk)
        olat = (acc * (1.0 / den)).astype(BF16)
        outs = [jnp.dot(wvt_ref[MLA_V_DIM * h: MLA_V_DIM * (h + 1), :], olat[:, tq * h: tq * (h + 1)],
                        preferred_element_type=F32) for h in range(MLA_HEADS)]
        for t in range(MLA_HEADS // 2):
            o_ref[0, :, LANES * t: LANES * (t + 1)] = jnp.concatenate(outs[2 * t: 2 * t + 2], axis=0).T.astype(BF16)

    _attn_pipeline(step, s0_ref, s1_ref, m0_ref, m1_ref)


def _mla_attention(qabs, qpe, wvt, ks, vts):
    B, L, _ = qabs.shape
    T = sum(k.shape[1] for k in ks)
    tq = min(128, L)
    nq = L // tq
    M = MLA_HEADS * tq
    last = B * nq - 1
    new = lambda g: jnp.minimum(g, last)
    old = lambda g: jnp.maximum(g - 1, 0)
    W = MLA_HEADS * MLA_V_DIM
    return pl.pallas_call(
        functools.partial(_mla_kernel, nseg=len(ks), tk=256),
        grid=(B * nq + 1,),
        in_specs=[pl.BlockSpec((1, tq, qabs.shape[2]), lambda g: (new(g) // nq, new(g) % nq, 0)),
                  pl.BlockSpec((1, tq, qpe.shape[2]), lambda g: (new(g) // nq, new(g) % nq, 0)),
                  _const_spec(wvt.shape)]
        + [pl.BlockSpec((1,) + k.shape[1:], lambda g: (new(g) // nq, 0, 0)) for k in ks]
        + [pl.BlockSpec((1,) + v.shape[1:], lambda g: (old(g) // nq, 0, 0)) for v in vts],
        out_specs=pl.BlockSpec((1, tq, W), lambda g: (old(g) // nq, old(g) % nq, 0)),
        out_shape=jax.ShapeDtypeStruct((B, L, W), BF16),
        scratch_shapes=[pltpu.VMEM((1, T, M), F32), pltpu.VMEM((1, T, M), F32),
                        pltpu.VMEM((1, SUBLANES, M), F32), pltpu.VMEM((1, SUBLANES, M), F32)],
        compiler_params=_cparams("arbitrary"),
        name="mla_attention",
    )(qabs, qpe, wvt, *ks, *vts)


def _outffn_kernel(x_ref, yh_ref, yg_ref, ym_ref, mod_ref, g2_ref, wo_ref, w1_ref, w3_ref, w2_ref,
                   gf_ref, o_ref, *, hchunk, final):
    x = x_ref[0]
    mix = jnp.concatenate([yh_ref[0].T.astype(BF16), yg_ref[0], ym_ref[0]], axis=1)
    y = jnp.dot(mix, wo_ref[...], preferred_element_type=F32)
    x1 = x + mod_ref[0, 2:3, :] * y
    h = x1 * lax.rsqrt(jnp.mean(x1 * x1, axis=-1, keepdims=True) + NORM_EPS) * g2_ref[...]
    h = (h * (1.0 + mod_ref[0, 4:5, :]) + mod_ref[0, 3:4, :]).astype(BF16)
    f = jnp.zeros_like(x)
    hidden = w1_ref.shape[1]
    for lo in range(0, hidden, hchunk):
        cols = slice(lo, min(lo + hchunk, hidden))
        a = jnp.dot(h, w1_ref[:, cols], preferred_element_type=F32)
        b = jnp.dot(h, w3_ref[:, cols], preferred_element_type=F32)
        g = (a * jax.nn.sigmoid(a) * b).astype(BF16)
        f = f + jnp.dot(g, w2_ref[cols, :], preferred_element_type=F32)
    x2 = x1 + mod_ref[0, 5:6, :] * f
    if final:
        x2 = x2 * lax.rsqrt(jnp.mean(x2 * x2, axis=-1, keepdims=True) + NORM_EPS) * gf_ref[...]
    o_ref[0] = x2


def _outffn(x, yh_t, yg, ym, mod, lw, final_g, final):
    B, L, D = x.shape
    tm = min(512, L)
    hidden = lw["w1"].shape[1]
    consts = [lw["g2"], lw["w_out"], lw["w1"], lw["w3"], lw["w2"], final_g]
    kern = functools.partial(_outffn_kernel, hchunk=6 * MXU_TILE, final=final)
    tok = lambda w: pl.BlockSpec((1, tm, w), lambda b, i: (b, i, 0))
    return pl.pallas_call(
        kern,
        grid=(B, L // tm),
        in_specs=[tok(D), pl.BlockSpec((1, HY_W, tm), lambda b, i: (b, 0, i)),
                  tok(yg.shape[2]), tok(ym.shape[2]),
                  pl.BlockSpec((1, N_MOD, D), lambda b, i: (b, 0, 0))]
        + [_const_spec(a.shape) for a in consts],
        out_specs=tok(D),
        out_shape=jax.ShapeDtypeStruct((B, L, D), F32),
        compiler_params=_cparams("parallel", "parallel"),
        name="outffn",
    )(x, yh_t, yg, ym, mod, *consts)


def _rope_tables(L, with_pos):
    lane = np.arange(LANES)

    def pattern(dim, off, width):
        loc = (lane - off) % dim
        half = dim // 2
        active = (lane >= off) & (lane < off + width)
        use_col = loc >= half
        fi = (loc % half) % (half // 2)
        first = (loc % half) < (half // 2)
        return active, use_col, fi, first, half

    def tables(dim, off, width):
        active, use_col, fi, first, half = pattern(dim, off, width)
        if not with_pos:
            one = jnp.ones((L, LANES), F32)
            zero = jnp.zeros((L, LANES), F32)
            return [one, zero, zero]
        row = jnp.repeat(jnp.arange(L // GRID_W, dtype=jnp.int32), GRID_W).astype(F32)
        col = jnp.tile(jnp.arange(GRID_W, dtype=jnp.int32), L // GRID_W).astype(F32)
        inv = ROPE_THETA ** (-jnp.arange(0, half, 2, dtype=F32) / half)
        pos = jnp.where(jnp.asarray(use_col)[None, :], col[:, None], row[:, None])
        ang = pos * inv[jnp.asarray(fi)][None, :]
        act = jnp.asarray(active)[None, :]
        fst = jnp.asarray(first)[None, :]
        cos = jnp.where(act, jnp.cos(ang), 1.0)
        sin = jnp.sin(ang)
        s1 = jnp.where(act & fst, -sin, 0.0)
        s2 = jnp.where(act & ~fst, sin, 0.0)
        return [cos, s1, s2]

    return tables(GQA_HEAD_DIM, 0, LANES) + tables(MLA_ROPE_DIM, 0, LANES)


def _gqa_tile_order():
    per = GQA_HEADS // GQA_KV_HEADS
    return [h for t in range(per) for h in (t, per + t)]


def _layer_weights(l, p):
    hd = GQA_HEAD_DIM
    order = _gqa_tile_order()
    qperm = np.concatenate([np.arange(hd) + hd * h for h in order])
    w_in = p["w_in"][l]
    w_in = jnp.concatenate([w_in[:, :IN_OFF[1]], w_in[:, IN_OFF[1]:IN_OFF[2]][:, qperm], w_in[:, IN_OFF[2]:]], axis=1)
    w_in = jnp.pad(w_in, ((0, 0), (0, IN_WIDTH_PAD - w_in.shape[1]))).astype(BF16)

    wq = p["mla_w_uq"][l].reshape(-1, MLA_HEADS, MLA_QK_DIM)
    rank_q = wq.shape[0]
    wq_pe = wq[:, :, MLA_NOPE_DIM:].reshape(rank_q, MLA_HEADS * MLA_ROPE_DIM)
    wq = jnp.concatenate([wq[:, :, :MLA_NOPE_DIM].reshape(rank_q, MLA_HEADS * MLA_NOPE_DIM),
                          jnp.pad(wq_pe, ((0, 0), (0, MLA_PE_LANES - wq_pe.shape[1])))], axis=1).astype(BF16)

    wkv = p["mla_w_ukv"][l].reshape(-1, MLA_HEADS, MLA_NOPE_DIM + MLA_V_DIM)
    latent = wkv.shape[0]
    wk_t = jnp.transpose(wkv[:, :, :MLA_NOPE_DIM], (1, 2, 0))
    eye = jnp.asarray(np.eye(MLA_HEADS, dtype=np.float32))
    wabs = (wk_t[:, :, None, :] * eye[:, None, :, None]).reshape(MLA_HEADS * MLA_NOPE_DIM, MLA_HEADS * latent)
    wvt = jnp.transpose(wkv[:, :, MLA_NOPE_DIM:], (1, 2, 0)).reshape(MLA_HEADS * MLA_V_DIM, latent)
    krep = np.zeros((LANES, LANES), np.float32)
    for s in range(LANES // MLA_ROPE_DIM):
        krep[np.arange(MLA_ROPE_DIM), s * MLA_ROPE_DIM + np.arange(MLA_ROPE_DIM)] = 1.0

    w_out = p["w_out"][l]
    g0 = HY_W
    g1 = g0 + GQA_HEADS * hd
    w_out = jnp.concatenate([w_out[:g0], w_out[g0:g1][qperm], w_out[g1:]], axis=0).astype(BF16)

    e2 = np.kron(np.eye(LANES // hd, dtype=np.float32), np.ones((hd, hd), np.float32))
    two = lambda g: jnp.tile(g, LANES // hd)[None, :]
    return {
        "g1": p["norm1_g"][l][None, :], "g2": p["norm2_g"][l][None, :],
        "w_in": w_in, "e2": jnp.asarray(e2, BF16),
        "gq": two(p["gqa_q_g"][l]), "gk": two(p["gqa_k_g"][l]),
        "gmq": p["mla_q_g"][l][None, :], "gmkv": p["mla_kv_g"][l][None, :],
        "wq": wq, "wabs": wabs.astype(BF16), "wvt": wvt.astype(BF16), "krep": jnp.asarray(krep, BF16),
        "w_out": w_out,
        "w1": p["ffn_w1"][l].astype(BF16), "w3": p["ffn_w3"][l].astype(BF16), "w2": p["ffn_w2"][l].astype(BF16),
        "conv_w": p["hy_conv_w"][l], "conv_b": p["hy_conv_b"][l][None, :], "hy_bias": p["hy_bias"][l],
    }


def kernel(x, c, ctx, c_ctx, mod_w, mod_b, norm1_g, norm2_g, w_in, hy_conv_w, hy_conv_b, hy_filt_w1, hy_filt_b1, hy_filt_w2, hy_filt_b2, hy_filt_w3, hy_filt_freq, hy_bias, gqa_q_g, gqa_k_g, mla_q_g, mla_kv_g, mla_w_uq, mla_w_ukv, w_out, ffn_w1, ffn_w3, ffn_w2, final_g):
    p = dict(norm1_g=norm1_g, norm2_g=norm2_g, w_in=w_in, hy_conv_w=hy_conv_w, hy_conv_b=hy_conv_b,
             hy_bias=hy_bias, gqa_q_g=gqa_q_g, gqa_k_g=gqa_k_g, mla_q_g=mla_q_g, mla_kv_g=mla_kv_g,
             mla_w_uq=mla_w_uq, mla_w_ukv=mla_w_ukv, w_out=w_out, ffn_w1=ffn_w1, ffn_w3=ffn_w3, ffn_w2=ffn_w2)
    B, L, D = x.shape
    Lc = ctx.shape[1]
    depth = mod_w.shape[0]

    pad_rows = (-(B + 1)) % 8
    c_all = jnp.concatenate([c, c_ctx[None, :], jnp.zeros((pad_rows, D), F32)], axis=0)
    mod = _modulation(c_all, mod_w, mod_b).reshape(depth, -1, N_MOD, D)

    tabs_lat = _rope_tables(L, True)
    tabs_ctx = _rope_tables(Lc, False)
    final_g2 = final_g[None, :]

    x_lat, x_ctx = x, ctx
    for l in range(depth):
        last = l == depth - 1
        lw = _layer_weights(l, p)
        filt = (hy_filt_w1[l], hy_filt_b1[l], hy_filt_w2[l], hy_filt_b2[l], hy_filt_w3[l], hy_filt_freq[l])
        mod_lat = mod[l, :B]
        mod_ctx = jnp.broadcast_to(mod[l, B][None], (B, N_MOD, D))

        ut_c, q_c, k_c, vt_c, qa_c, qp_c, km_c, vmt_c = _inproj(x_ctx, mod_ctx, lw, tabs_ctx)
        ut, q, k, vt, qa, qp, km, vmt = _inproj(x_lat, mod_lat, lw, tabs_lat)

        if not last:
            yh_c = _longconv(ut_c, _hyena_filters(Lc, *filt), lw["hy_bias"])
            yg_c = _gqa_attention(q_c, [k_c], [vt_c])
            ym_c = _mla_attention(qa_c, qp_c, lw["wvt"], [km_c], [vmt_c])
            x_ctx_next = _outffn(x_ctx, yh_c, yg_c, ym_c, mod_ctx, lw, final_g2, False)

        yh = _longconv(ut, _hyena_filters(L, *filt), lw["hy_bias"])
        yg = _gqa_attention(q, [k, k_c], [vt, vt_c])
        ym = _mla_attention(qa, qp, lw["wvt"], [km, km_c], [vmt, vmt_c])
        x_lat = _outffn(x_lat, yh, yg, ym, mod_lat, lw, final_g2, last)
        if not last:
            x_ctx = x_ctx_next
    return x_lat
```

```python
import functools
import math

import jax
import jax.numpy as jnp
import numpy as np
from jax import lax
from jax.experimental import pallas as pl
from jax.experimental.pallas import tpu as pltpu

F32 = jnp.float32
BF16 = jnp.bfloat16

N_MOD = 6
GRID_W = 64
NORM_EPS = 1e-6
ROPE_THETA = 10000.0
HY_W = 256
HY_ORDER = 2
HY_BANDS = 16
HY_TARGET = 1e-2
HY_FAST_PCT = 0.3
HY_SLOW_PCT = 1.5
GQA_HEADS = 6
GQA_KV_HEADS = 2
GQA_HEAD_DIM = 64
MLA_HEADS = 6
MLA_NOPE_DIM = 64
MLA_ROPE_DIM = 32
MLA_V_DIM = 64
MLA_QK_DIM = MLA_NOPE_DIM + MLA_ROPE_DIM
IN_SPLITS = (768, 384, 128, 128, 256, 128, 32)
IN_OFF = tuple(int(v) for v in np.cumsum((0,) + IN_SPLITS))
IN_WIDTH_PAD = 1920

LANES = 128
SUBLANES = 8
BF16_ROWS = 16
MXU_TILE = 256
MLA_PE_LANES = -(-MLA_HEADS * MLA_ROPE_DIM // LANES) * LANES
VMEM_LIMIT = 56 * 1024 * 1024
CONV_BLK = 256
CONV_CH = 8

LOG2E = 1.4426950408889634
HIGHEST = lax.Precision.HIGHEST


def _cparams(*sem):
    return pltpu.CompilerParams(dimension_semantics=sem, vmem_limit_bytes=VMEM_LIMIT)


def _const_spec(shape):
    zeros = (0,) * len(shape)
    return pl.BlockSpec(shape, lambda *_: zeros, pipeline_mode=pl.Buffered(1))


def _mod_kernel(c_ref, w_ref, b_ref, o_ref):
    c = c_ref[...]
    sc = c * jax.nn.sigmoid(c)
    o_ref[0] = jnp.dot(sc, w_ref[0], precision=HIGHEST, preferred_element_type=F32) + b_ref[0]


def _modulation(c_all, mod_w, mod_b):
    depth, d, n = mod_w.shape
    rows = c_all.shape[0]
    tn = 1536
    return pl.pallas_call(
        _mod_kernel,
        grid=(depth, n // tn),
        in_specs=[
            pl.BlockSpec((rows, d), lambda l, j: (0, 0)),
            pl.BlockSpec((1, d, tn), lambda l, j: (l, 0, j)),
            pl.BlockSpec((1, 1, tn), lambda l, j: (l, 0, j)),
        ],
        out_specs=pl.BlockSpec((1, rows, tn), lambda l, j: (l, 0, j)),
        out_shape=jax.ShapeDtypeStruct((depth, rows, n), F32),
        compiler_params=_cparams("parallel", "parallel"),
        name="modulation",
    )(c_all, mod_w, mod_b.reshape(depth, 1, n))


def _filter_kernel(zt_ref, w1t_ref, b1_ref, w2t_ref, b2_ref, fr_ref, w3f_ref, w3b_ref,
                   trow_ref, sel_ref, delta_ref, o_ref, h_ref):
    @pl.when(pl.program_id(0) == 0)
    def _():
        fr = fr_ref[...]
        h = jnp.sin(fr * (jnp.dot(w1t_ref[...], zt_ref[...], precision=HIGHEST,
                                  preferred_element_type=F32) + b1_ref[...]))
        h_ref[...] = jnp.sin(fr * (jnp.dot(w2t_ref[...], h, precision=HIGHEST,
                                           preferred_element_type=F32) + b2_ref[...]))

    h = h_ref[...]
    h_hi = h.astype(BF16)
    h_lo = (h - h_hi.astype(F32)).astype(BF16)

    def dot3(w):
        w_hi = w.astype(BF16)
        w_lo = (w - w_hi.astype(F32)).astype(BF16)
        return (jnp.dot(w_hi, h_hi, preferred_element_type=F32)
                + (jnp.dot(w_hi, h_lo, preferred_element_type=F32) + jnp.dot(w_lo, h_hi, preferred_element_type=F32)))

    fwd = dot3(w3f_ref[...])
    bwd = dot3(w3b_ref[...])
    sel = sel_ref[...]
    k = jnp.where(sel > 0.0, fwd, jnp.where(sel < 0.0, bwd, 0.0))
    k = k * jnp.exp(-delta_ref[...] * trow_ref[...])
    nrm = jnp.sum(jnp.abs(k), axis=1, keepdims=True)
    o_ref[0] = k / nrm


def _filter_features(L):
    t = np.linspace(0.0, 1.0, L, dtype=np.float32)
    wpos = np.float32(2.0 * math.pi) * np.arange(L, dtype=np.float32) / np.float32(L)
    f = np.linspace(1e-4, HY_BANDS - 1, HY_BANDS, dtype=np.float32)
    ang = (f[None, :] * wpos[:, None]).astype(np.float64)
    z = np.concatenate([t[:, None], np.cos(ang), -np.sin(ang)], axis=1).astype(np.float32)
    n = np.arange(2 * L)
    lag = np.where(n <= L, np.minimum(n, L - 1), 2 * L - n)
    sel = np.where(n < L, 1.0, np.where(n > L, -1.0, 0.0)).astype(np.float32)
    zt = np.zeros((LANES, 2 * L), np.float32)
    zt[:z.shape[1]] = z[lag].T
    return zt, t[lag][None, :], sel[None, :]


def _hyena_filters(L, w1, b1, w2, b2, w3, freq):
    P = 2 * L
    zt, trow, sel = _filter_features(L)
    nf = w1.shape[1]
    w1t = jnp.pad(w1, ((0, LANES - w1.shape[0]), (0, 0))).T
    col = lambda a: a[:, None]
    deltas = np.abs(np.linspace(math.log(HY_TARGET) / HY_FAST_PCT, math.log(HY_TARGET) / HY_SLOW_PCT,
                                HY_W, dtype=np.float32))[:, None]
    halves = HY_W // LANES
    cb = 2 * halves
    whole = lambda shape: pl.BlockSpec(shape, lambda s: (0, 0))
    return pl.pallas_call(
        _filter_kernel,
        grid=(HY_ORDER * halves,),
        in_specs=[
            whole((LANES, P)), whole((nf, LANES)), whole((nf, 1)), whole((nf, nf)), whole((nf, 1)),
            whole((nf, 1)),
            pl.BlockSpec((LANES, nf), lambda s: ((s // halves) * cb + s % halves, 0)),
            pl.BlockSpec((LANES, nf), lambda s: ((s // halves) * cb + halves + s % halves, 0)),
            whole((1, P)), whole((1, P)),
            pl.BlockSpec((LANES, 1), lambda s: (s % halves, 0)),
        ],
        out_specs=pl.BlockSpec((1, LANES, P), lambda s: (s // halves, s % halves, 0)),
        out_shape=jax.ShapeDtypeStruct((HY_ORDER, HY_W, P), F32),
        scratch_shapes=[pltpu.VMEM((nf, P), F32)],
        compiler_params=_cparams("arbitrary"),
        name="hyena_filters",
    )(zt, w1t, col(b1), w2.T, col(b2), col(freq), w3.T, w3.T, trow, sel, deltas)


def _rope(t, c_ref, s1_ref, s2_ref, sh):
    return (t * c_ref[...] + pltpu.roll(t, LANES - sh, 1) * s1_ref[...]
            + pltpu.roll(t, sh, 1) * s2_ref[...])


def _inproj_kernel(x_ref, xp_ref, xn_ref, mod_ref, g1_ref, w_ref, cw_ref, cb_ref,
                   e2_ref, gq_ref, gk_ref, gmq_ref, gmkv_ref, wq_ref, wabs_ref, krep_ref,
                   cg_ref, s1g_ref, s2g_ref, cm_ref, s1m_ref, s2m_ref,
                   ut_ref, q_ref, k_ref, vt_ref, qabs_ref, qpe_ref, km_ref, vmt_ref, *, q_scale, qm_scale):
    shift = mod_ref[0, 0:1, :]
    scale = mod_ref[0, 1:2, :]

    def norm_mod(xx):
        hh = xx * lax.rsqrt(jnp.mean(xx * xx, axis=-1, keepdims=True) + NORM_EPS) * g1_ref[...]
        return (hh * (1.0 + scale) + shift).astype(BF16)

    u = jnp.dot(norm_mod(x_ref[0]), w_ref[...], preferred_element_type=F32)

    hy_w = IN_OFF[1]
    tm = u.shape[0]
    i, n_i = pl.program_id(0), pl.num_programs(0)
    edge = jnp.dot(norm_mod(jnp.concatenate([xp_ref[0], xn_ref[0]], axis=0)), w_ref[:, 0:hy_w],
                   preferred_element_type=F32)
    before = jnp.where(i > 0, edge[SUBLANES - 1:SUBLANES, :], 0.0)
    after = jnp.where(i < n_i - 1, edge[SUBLANES:SUBLANES + 1, :], 0.0)
    uh = u[:, 0:hy_w]
    row = lax.broadcasted_iota(jnp.int32, uh.shape, 0)
    prev = jnp.where(row == 0, before, pltpu.roll(uh, 1, 0))
    nxt = jnp.where(row == tm - 1, after, pltpu.roll(uh, tm - 1, 0))
    y = prev * cw_ref[0:1, :] + uh * cw_ref[1:2, :] + nxt * cw_ref[2:3, :] + cb_ref[...]
    for p in range(hy_w // HY_W):
        ut_ref[p, 0] = y[:, HY_W * p: HY_W * (p + 1)].T

    e2 = e2_ref[...]

    def head_norm(t, g):
        sq = t * t
        hi = sq.astype(BF16)
        lo = (sq - hi.astype(F32)).astype(BF16)
        ss = (jnp.dot(hi, e2, preferred_element_type=F32) + jnp.dot(lo, e2, preferred_element_type=F32))
        return t * lax.rsqrt(ss * (1.0 / GQA_HEAD_DIM) + NORM_EPS) * g

    for i in range(GQA_HEADS * GQA_HEAD_DIM // LANES):
        t = u[:, IN_OFF[1] + LANES * i: IN_OFF[1] + LANES * (i + 1)]
        t = _rope(head_norm(t, gq_ref[...]), cg_ref, s1g_ref, s2g_ref, GQA_HEAD_DIM // 4)
        q_ref[0, :, LANES * i: LANES * (i + 1)] = (t * q_scale).astype(BF16)
    t = u[:, IN_OFF[2]:IN_OFF[3]]
    k_ref[0] = _rope(head_norm(t, gk_ref[...]), cg_ref, s1g_ref, s2g_ref, GQA_HEAD_DIM // 4).astype(BF16)
    vt_ref[0] = u[:, IN_OFF[3]:IN_OFF[4]].T.astype(BF16)

    mq = u[:, IN_OFF[4]:IN_OFF[5]]
    mqn = mq * lax.rsqrt(jnp.mean(mq * mq, axis=-1, keepdims=True) + NORM_EPS) * gmq_ref[...]
    qm = jnp.dot(mqn.astype(BF16), wq_ref[...], preferred_element_type=F32)
    n_nope = MLA_HEADS * MLA_NOPE_DIM
    qabs = jnp.dot(qm[:, :n_nope].astype(BF16), wabs_ref[...], preferred_element_type=F32)
    qabs_ref[0] = (qabs * qm_scale).astype(BF16)
    for i in range(qpe_ref.shape[2] // LANES):
        t = _rope(qm[:, n_nope + LANES * i: n_nope + LANES * (i + 1)], cm_ref, s1m_ref, s2m_ref, MLA_ROPE_DIM // 4)
        qpe_ref[0, :, LANES * i: LANES * (i + 1)] = (t * qm_scale).astype(BF16)

    mkv = u[:, IN_OFF[5]:IN_OFF[6]]
    kvn = mkv * lax.rsqrt(jnp.mean(mkv * mkv, axis=-1, keepdims=True) + NORM_EPS) * gmkv_ref[...]
    kpe = _rope(u[:, IN_OFF[6]:IN_OFF[6] + LANES], cm_ref, s1m_ref, s2m_ref, MLA_ROPE_DIM // 4).astype(BF16)
    kpe4 = jnp.dot(kpe, krep_ref[...], preferred_element_type=F32).astype(BF16)
    km_ref[0] = jnp.concatenate([kvn.astype(BF16), kpe4], axis=1)
    vmt_ref[0] = kvn.T.astype(BF16)


def _inproj(x, mod, lw, tabs):
    B, L, D = x.shape
    tm = min(1024, L)
    tab_spec = pl.BlockSpec((tm, LANES), lambda i, b: (i, 0))
    tok = lambda w: pl.BlockSpec((1, tm, w), lambda i, b: (b, i, 0))
    kern = functools.partial(_inproj_kernel,
                             q_scale=GQA_HEAD_DIM ** -0.5 * LOG2E, qm_scale=MLA_QK_DIM ** -0.5 * LOG2E)
    consts = [lw["g1"], lw["w_in"], lw["conv_w"], lw["conv_b"], lw["e2"], lw["gq"], lw["gk"], lw["gmq"], lw["gmkv"],
              lw["wq"], lw["wabs"], lw["krep"]]
    latent = lw["wabs"].shape[1] // MLA_HEADS
    out_w = [(IN_SPLITS[1], BF16, False), (IN_SPLITS[2], BF16, False),
             (IN_SPLITS[3], BF16, True), (MLA_HEADS * latent, BF16, False), (MLA_PE_LANES, BF16, False),
             (latent + LANES, BF16, False), (latent, BF16, True)]
    tok_t = lambda w: pl.BlockSpec((1, w, tm), lambda i, b: (b, 0, i))
    nparts = IN_SPLITS[0] // HY_W
    per = tm // SUBLANES
    x_before = pl.BlockSpec((1, SUBLANES, D), lambda i, b: (b, jnp.maximum(i * per - 1, 0), 0))
    x_after = pl.BlockSpec((1, SUBLANES, D), lambda i, b: (b, jnp.minimum((i + 1) * per, L // SUBLANES - 1), 0))
    return pl.pallas_call(
        kern,
        grid=(L // tm, B),
        in_specs=[tok(D), x_before, x_after, pl.BlockSpec((1, N_MOD, D), lambda i, b: (b, 0, 0))]
        + [_const_spec(a.shape) for a in consts] + [tab_spec] * len(tabs),
        out_specs=[pl.BlockSpec((nparts, 1, HY_W, tm), lambda i, b: (0, b, 0, i))]
        + [tok_t(w) if tr else tok(w) for w, _, tr in out_w],
        out_shape=[jax.ShapeDtypeStruct((nparts, B, HY_W, L), F32)]
        + [jax.ShapeDtypeStruct((B, w, L) if tr else (B, L, w), dt) for w, dt, tr in out_w],
        compiler_params=_cparams("parallel", "parallel"),
        name="inproj",
    )(x, x, x, mod, *consts, *tabs)


def _longconv_kernel(bias_ref, kf_ref, u_ref, o_ref, s0_ref, s1_ref, acc_ref, x_ref, y_ref, *, nblk):
    g = pl.program_id(0)
    B, CH = u_ref.shape[1], u_ref.shape[2]
    P = s0_ref.shape[1]
    for p in range(3):
        x_ref[p] = jnp.swapaxes(u_ref[p], 0, 1)

    def to_rows(z):
        return jnp.concatenate([z[:, j * CONV_BLK:(j + 1) * CONV_BLK] for j in range(nblk)], axis=0)

    def conv(zin, o, c, s_ref):
        krow = kf_ref[o, pl.ds(c, 1), :]
        s_ref[...] = pltpu.roll(jnp.broadcast_to(krow, (CONV_BLK, P)), 0, 1,
                                stride=1, stride_axis=0).astype(BF16)
        zb = zin.astype(BF16)
        acc_ref[...] = jnp.zeros_like(acc_ref)
        for d in range(-(nblk - 1), nblk):
            j0 = max(0, -d)
            n = nblk - abs(d)
            i0 = j0 + d
            col = (d * CONV_BLK) % P
            acc_ref[B * i0: B * (i0 + n), :] += jnp.dot(
                zb[B * j0: B * (j0 + n), :], s_ref[:, col: col + CONV_BLK],
                preferred_element_type=F32)
        return acc_ref[...] + zin * bias_ref[o, g * CH + c]

    def channel(c, carry):
        z1 = to_rows(x_ref[1, c]) * conv(to_rows(x_ref[0, c]), 0, c, s0_ref)
        y = to_rows(x_ref[2, c]) * conv(z1, 1, c, s1_ref)
        y_ref[c] = jnp.concatenate([y[B * i: B * (i + 1)] for i in range(nblk)], axis=1)
        return carry

    lax.fori_loop(0, CH, channel, 0, unroll=True)
    o_ref[...] = jnp.swapaxes(y_ref[...], 0, 1)


def _longconv(ut, kf, hy_bias):
    _, B, C, L = ut.shape
    P = kf.shape[-1]
    nblk = L // CONV_BLK
    return pl.pallas_call(
        functools.partial(_longconv_kernel, nblk=nblk),
        grid=(C // CONV_CH,),
        in_specs=[
            pl.BlockSpec(memory_space=pltpu.SMEM),
            pl.BlockSpec((HY_ORDER, CONV_CH, P), lambda g: (0, g, 0)),
            pl.BlockSpec((3, B, CONV_CH, L), lambda g: (0, 0, g, 0)),
        ],
        out_specs=pl.BlockSpec((B, CONV_CH, L), lambda g: (0, g, 0)),
        out_shape=jax.ShapeDtypeStruct((B, C, L), F32),
        scratch_shapes=[pltpu.VMEM((CONV_BLK, P), BF16), pltpu.VMEM((CONV_BLK, P), BF16),
                        pltpu.VMEM((nblk * B, CONV_BLK), F32),
                        pltpu.VMEM((3, CONV_CH, B, L), F32), pltpu.VMEM((CONV_CH, B, L), F32)],
        compiler_params=_cparams("parallel"),
        name="hyena_longconv",
    )(hy_bias, kf, ut)


def _attn_units_step(qs, k_refs, vt_refs, s_new, s_old, m_new, m_old, koffs, voffs, vrows, tk):
    nu = len(qs)
    M = qs[0].shape[0]
    dn = (((1,), (1,)), ((), ()))
    row = 0
    for k_ref in k_refs:
        n = k_ref.shape[1]
        for u in range(nu):
            s_new[u, row:row + n, :] = lax.dot_general(k_ref[0, :, koffs[u]:koffs[u] + qs[u].shape[1]], qs[u], dn,
                                                       preferred_element_type=F32)
        row += n
    mo = [m_old[u, 0:1, :] for u in range(nu)]
    mx = [jnp.full((SUBLANES, M), -jnp.inf, F32) for _ in range(nu)]
    ones = jnp.ones((BF16_ROWS, tk), BF16)
    acc = [jnp.zeros((vrows + BF16_ROWS, M), F32) for _ in range(nu)]
    row = 0
    for vt_ref in vt_refs:
        for c in range(vt_ref.shape[2] // tk):
            rows = slice(row, row + tk)
            row += tk
            for u in range(nu):
                p = jnp.exp2(s_old[u, rows, :] - mo[u]).astype(BF16)
                vt = jnp.concatenate([vt_ref[0, voffs[u]:voffs[u] + vrows, c * tk:(c + 1) * tk], ones], axis=0)
                acc[u] = acc[u] + jnp.dot(vt, p, preferred_element_type=F32)
                sn = s_new[u, rows, :]
                for r in range(tk // SUBLANES):
                    mx[u] = jnp.maximum(mx[u], sn[SUBLANES * r: SUBLANES * (r + 1), :])
    for u in range(nu):
        m_new[u] = jnp.broadcast_to(jnp.max(mx[u], axis=0, keepdims=True), (SUBLANES, M))
    return [a[:vrows, :] for a in acc], [a[vrows:vrows + 1, :] for a in acc]


def _attn_pipeline(step, s0_ref, s1_ref, m0_ref, m1_ref, e0_ref, e1_ref):
    g = pl.program_id(0)

    @pl.when(g == 0)
    def _():
        s1_ref[...] = jnp.zeros_like(s1_ref)
        m1_ref[...] = jnp.zeros_like(m1_ref)
        e1_ref[...] = jnp.ones_like(e1_ref)

    @pl.when(g % 2 == 0)
    def _():
        step(s0_ref, s1_ref, m0_ref, m1_ref, e0_ref, e1_ref)

    @pl.when(g % 2 == 1)
    def _():
        step(s1_ref, s0_ref, m1_ref, m0_ref, e1_ref, e0_ref)


def _attn_tile_maps(ntiles):
    scored = lambda g: jnp.minimum(g, ntiles - 1)
    valued = lambda g: jnp.clip(g - 1, 0, ntiles - 1)
    written = lambda g: jnp.maximum(g - 2, 0)
    return scored, valued, written


def _park(e_ref, acc, den):
    rows = acc.shape[0]
    e_ref[0:rows, :] = acc
    e_ref[rows:rows + SUBLANES, :] = jnp.broadcast_to(den, (SUBLANES, den.shape[1]))


def _gqa_kernel(*refs, nseg, tk):
    q_ref, k_refs, v_refs = refs[0], refs[1:1 + nseg], refs[1 + nseg:1 + 2 * nseg]
    o_ref, s0_ref, s1_ref, m0_ref, m1_ref, e0_ref, e1_ref = refs[1 + 2 * nseg:]
    tq = q_ref.shape[1]
    ntile = q_ref.shape[2] // LANES
    vrows = GQA_KV_HEADS * GQA_HEAD_DIM

    def step(s_new, s_old, m_new, m_old, e_new, e_old):
        o = e_old[0:vrows, :] * (1.0 / e_old[vrows:vrows + 1, :])
        top = lax.broadcasted_iota(jnp.int32, (vrows, tq), 0) < GQA_HEAD_DIM
        for t in range(ntile):
            oa = o[:, (2 * t) * tq:(2 * t + 1) * tq]
            ob = o[:, (2 * t + 1) * tq:(2 * t + 2) * tq]
            o_ref[0, :, LANES * t: LANES * (t + 1)] = jnp.where(top, oa, ob).T.astype(BF16)

        low = lax.broadcasted_iota(jnp.int32, (tq, LANES), 1) < GQA_HEAD_DIM
        zero = jnp.zeros((tq, LANES), BF16)
        qs = []
        for t in range(ntile):
            qt = q_ref[0, :, LANES * t: LANES * (t + 1)]
            qs.append(jnp.where(low, qt, zero))
            qs.append(jnp.where(low, zero, qt))
        (acc,), (den,) = _attn_units_step([jnp.concatenate(qs, axis=0)], k_refs, v_refs, s_new, s_old,
                                          m_new, m_old, (0,), (0,), vrows, tk)
        _park(e_new, acc, den)

    _attn_pipeline(step, s0_ref, s1_ref, m0_ref, m1_ref, e0_ref, e1_ref)


def _gqa_attention(q, ks, vts):
    B, L, W = q.shape
    T = sum(k.shape[1] for k in ks)
    tq = min(128, L)
    nq = L // tq
    M = GQA_HEADS * tq
    scored, valued, written = _attn_tile_maps(B * nq)
    erows = GQA_KV_HEADS * GQA_HEAD_DIM + SUBLANES
    return pl.pallas_call(
        functools.partial(_gqa_kernel, nseg=len(ks), tk=256),
        grid=(B * nq + 2,),
        in_specs=[pl.BlockSpec((1, tq, W), lambda g: (scored(g) // nq, scored(g) % nq, 0))]
        + [pl.BlockSpec((1, k.shape[1], LANES), lambda g: (scored(g) // nq, 0, 0)) for k in ks]
        + [pl.BlockSpec((1, LANES, v.shape[2]), lambda g: (valued(g) // nq, 0, 0)) for v in vts],
        out_specs=pl.BlockSpec((1, tq, W), lambda g: (written(g) // nq, written(g) % nq, 0)),
        out_shape=jax.ShapeDtypeStruct((B, L, W), BF16),
        scratch_shapes=[pltpu.VMEM((1, T, M), F32), pltpu.VMEM((1, T, M), F32),
                        pltpu.VMEM((1, SUBLANES, M), F32), pltpu.VMEM((1, SUBLANES, M), F32),
                        pltpu.VMEM((erows, M), F32), pltpu.VMEM((erows, M), F32)],
        compiler_params=_cparams("arbitrary"),
        name="gqa_attention",
    )(q, *ks, *vts)


def _mla_kernel(*refs, nseg, tk):
    qabs_ref, qpe_ref, wvt_ref = refs[0:3]
    k_refs, v_refs = refs[3:3 + nseg], refs[3 + nseg:3 + 2 * nseg]
    o_ref, s0_ref, s1_ref, m0_ref, m1_ref, e0_ref, e1_ref = refs[3 + 2 * nseg:]
    tq = qabs_ref.shape[1]
    latent = v_refs[0].shape[1]
    slots = LANES // MLA_ROPE_DIM

    def step(s_new, s_old, m_new, m_old, e_new, e_old):
        olat = (e_old[0:latent, :] * (1.0 / e_old[latent:latent + 1, :])).astype(BF16)
        outs = [jnp.dot(wvt_ref[MLA_V_DIM * h: MLA_V_DIM * (h + 1), :], olat[:, tq * h: tq * (h + 1)],
                        preferred_element_type=F32) for h in range(MLA_HEADS)]
        for t in range(MLA_HEADS // 2):
            o_ref[0, :, LANES * t: LANES * (t + 1)] = jnp.concatenate(outs[2 * t: 2 * t + 2], axis=0).T.astype(BF16)

        lane = lax.broadcasted_iota(jnp.int32, (tq, LANES), 1)
        zero = jnp.zeros((tq, LANES), BF16)
        qs = []
        for h in range(MLA_HEADS):
            pe = qpe_ref[0, :, LANES * (h // slots): LANES * (h // slots + 1)]
            mine = (lane >= MLA_ROPE_DIM * (h % slots)) & (lane < MLA_ROPE_DIM * (h % slots + 1))
            qs.append(jnp.concatenate([qabs_ref[0, :, latent * h: latent * (h + 1)], jnp.where(mine, pe, zero)], axis=1))
        (acc,), (den,) = _attn_units_step([jnp.concatenate(qs, axis=0)], k_refs, v_refs, s_new, s_old,
                                          m_new, m_old, (0,), (0,), latent, tk)
        _park(e_new, acc, den)

    _attn_pipeline(step, s0_ref, s1_ref, m0_ref, m1_ref, e0_ref, e1_ref)


def _mla_attention(qabs, qpe, wvt, ks, vts):
    B, L, _ = qabs.shape
    T = sum(k.shape[1] for k in ks)
    tq = min(128, L)
    nq = L // tq
    M = MLA_HEADS * tq
    scored, valued, written = _attn_tile_maps(B * nq)
    W = MLA_HEADS * MLA_V_DIM
    erows = vts[0].shape[1] + SUBLANES
    return pl.pallas_call(
        functools.partial(_mla_kernel, nseg=len(ks), tk=256),
        grid=(B * nq + 2,),
        in_specs=[pl.BlockSpec((1, tq, qabs.shape[2]), lambda g: (scored(g) // nq, scored(g) % nq, 0)),
                  pl.BlockSpec((1, tq, qpe.shape[2]), lambda g: (scored(g) // nq, scored(g) % nq, 0)),
                  _const_spec(wvt.shape)]
        + [pl.BlockSpec((1,) + k.shape[1:], lambda g: (scored(g) // nq, 0, 0)) for k in ks]
        + [pl.BlockSpec((1,) + v.shape[1:], lambda g: (valued(g) // nq, 0, 0)) for v in vts],
        out_specs=pl.BlockSpec((1, tq, W), lambda g: (written(g) // nq, written(g) % nq, 0)),
        out_shape=jax.ShapeDtypeStruct((B, L, W), BF16),
        scratch_shapes=[pltpu.VMEM((1, T, M), F32), pltpu.VMEM((1, T, M), F32),
                        pltpu.VMEM((1, SUBLANES, M), F32), pltpu.VMEM((1, SUBLANES, M), F32),
                        pltpu.VMEM((erows, M), F32), pltpu.VMEM((erows, M), F32)],
        compiler_params=_cparams("arbitrary"),
        name="mla_attention",
    )(qabs, qpe, wvt, *ks, *vts)


def _outffn_kernel(x_ref, yh_ref, yg_ref, ym_ref, mod_ref, g2_ref, wo_ref, w1_ref, w3_ref, w2_ref,
                   gf_ref, o_ref, *, hchunk, final):
    x = x_ref[0]
    mix = jnp.concatenate([yh_ref[0].T.astype(BF16), yg_ref[0], ym_ref[0]], axis=1)
    y = jnp.dot(mix, wo_ref[...], preferred_element_type=F32)
    x1 = x + mod_ref[0, 2:3, :] * y
    h = x1 * lax.rsqrt(jnp.mean(x1 * x1, axis=-1, keepdims=True) + NORM_EPS) * g2_ref[...]
    h = (h * (1.0 + mod_ref[0, 4:5, :]) + mod_ref[0, 3:4, :]).astype(BF16)
    f = jnp.zeros_like(x)
    hidden = w1_ref.shape[1]
    for lo in range(0, hidden, hchunk):
        cols = slice(lo, min(lo + hchunk, hidden))
        a = jnp.dot(h, w1_ref[:, cols], preferred_element_type=F32)
        b = jnp.dot(h, w3_ref[:, cols], preferred_element_type=F32)
        g = (a * jax.nn.sigmoid(a) * b).astype(BF16)
        f = f + jnp.dot(g, w2_ref[cols, :], preferred_element_type=F32)
    x2 = x1 + mod_ref[0, 5:6, :] * f
    if final:
        x2 = x2 * lax.rsqrt(jnp.mean(x2 * x2, axis=-1, keepdims=True) + NORM_EPS) * gf_ref[...]
    o_ref[0] = x2


def _outffn(x, yh_t, yg, ym, mod, lw, final_g, final):
    B, L, D = x.shape
    tm = min(512, L)
    hidden = lw["w1"].shape[1]
    consts = [lw["g2"], lw["w_out"], lw["w1"], lw["w3"], lw["w2"], final_g]
    kern = functools.partial(_outffn_kernel, hchunk=6 * MXU_TILE, final=final)
    tok = lambda w: pl.BlockSpec((1, tm, w), lambda b, i: (b, i, 0))
    return pl.pallas_call(
        kern,
        grid=(B, L // tm),
        in_specs=[tok(D), pl.BlockSpec((1, HY_W, tm), lambda b, i: (b, 0, i)),
                  tok(yg.shape[2]), tok(ym.shape[2]),
                  pl.BlockSpec((1, N_MOD, D), lambda b, i: (b, 0, 0))]
        + [_const_spec(a.shape) for a in consts],
        out_specs=tok(D),
        out_shape=jax.ShapeDtypeStruct((B, L, D), F32),
        compiler_params=_cparams("parallel", "parallel"),
        name="outffn",
    )(x, yh_t, yg, ym, mod, *consts)


def _rope_tables(L, with_pos):
    lane = np.arange(LANES)

    def pattern(dim, off, width):
        loc = (lane - off) % dim
        half = dim // 2
        active = (lane >= off) & (lane < off + width)
        use_col = loc >= half
        fi = (loc % half) % (half // 2)
        first = (loc % half) < (half // 2)
        return active, use_col, fi, first, half

    def tables(dim, off, width):
        active, use_col, fi, first, half = pattern(dim, off, width)
        if not with_pos:
            one = jnp.ones((L, LANES), F32)
            zero = jnp.zeros((L, LANES), F32)
            return [one, zero, zero]
        row = jnp.repeat(jnp.arange(L // GRID_W, dtype=jnp.int32), GRID_W).astype(F32)
        col = jnp.tile(jnp.arange(GRID_W, dtype=jnp.int32), L // GRID_W).astype(F32)
        inv = ROPE_THETA ** (-jnp.arange(0, half, 2, dtype=F32) / half)
        pos = jnp.where(jnp.asarray(use_col)[None, :], col[:, None], row[:, None])
        ang = pos * inv[jnp.asarray(fi)][None, :]
        act = jnp.asarray(active)[None, :]
        fst = jnp.asarray(first)[None, :]
        cos = jnp.where(act, jnp.cos(ang), 1.0)
        sin = jnp.sin(ang)
        s1 = jnp.where(act & fst, -sin, 0.0)
        s2 = jnp.where(act & ~fst, sin, 0.0)
        return [cos, s1, s2]

    return tables(GQA_HEAD_DIM, 0, LANES) + tables(MLA_ROPE_DIM, 0, LANES)


def _gqa_tile_order():
    per = GQA_HEADS // GQA_KV_HEADS
    return [h for t in range(per) for h in (t, per + t)]


def _layer_weights(l, p):
    hd = GQA_HEAD_DIM
    order = _gqa_tile_order()
    qperm = np.concatenate([np.arange(hd) + hd * h for h in order])
    w_in = p["w_in"][l]
    w_in = jnp.concatenate([w_in[:, :IN_OFF[1]], w_in[:, IN_OFF[1]:IN_OFF[2]][:, qperm], w_in[:, IN_OFF[2]:]], axis=1)
    w_in = jnp.pad(w_in, ((0, 0), (0, IN_WIDTH_PAD - w_in.shape[1]))).astype(BF16)

    wq = p["mla_w_uq"][l].reshape(-1, MLA_HEADS, MLA_QK_DIM)
    rank_q = wq.shape[0]
    wq_pe = wq[:, :, MLA_NOPE_DIM:].reshape(rank_q, MLA_HEADS * MLA_ROPE_DIM)
    wq = jnp.concatenate([wq[:, :, :MLA_NOPE_DIM].reshape(rank_q, MLA_HEADS * MLA_NOPE_DIM),
                          jnp.pad(wq_pe, ((0, 0), (0, MLA_PE_LANES - wq_pe.shape[1])))], axis=1).astype(BF16)

    wkv = p["mla_w_ukv"][l].reshape(-1, MLA_HEADS, MLA_NOPE_DIM + MLA_V_DIM)
    latent = wkv.shape[0]
    wk_t = jnp.transpose(wkv[:, :, :MLA_NOPE_DIM], (1, 2, 0))
    eye = jnp.asarray(np.eye(MLA_HEADS, dtype=np.float32))
    wabs = (wk_t[:, :, None, :] * eye[:, None, :, None]).reshape(MLA_HEADS * MLA_NOPE_DIM, MLA_HEADS * latent)
    wvt = jnp.transpose(wkv[:, :, MLA_NOPE_DIM:], (1, 2, 0)).reshape(MLA_HEADS * MLA_V_DIM, latent)
    krep = np.zeros((LANES, LANES), np.float32)
    for s in range(LANES // MLA_ROPE_DIM):
        krep[np.arange(MLA_ROPE_DIM), s * MLA_ROPE_DIM + np.arange(MLA_ROPE_DIM)] = 1.0

    w_out = p["w_out"][l]
    g0 = HY_W
    g1 = g0 + GQA_HEADS * hd
    w_out = jnp.concatenate([w_out[:g0], w_out[g0:g1][qperm], w_out[g1:]], axis=0).astype(BF16)

    e2 = np.kron(np.eye(LANES // hd, dtype=np.float32), np.ones((hd, hd), np.float32))
    two = lambda g: jnp.tile(g, LANES // hd)[None, :]
    return {
        "g1": p["norm1_g"][l][None, :], "g2": p["norm2_g"][l][None, :],
        "w_in": w_in, "e2": jnp.asarray(e2, BF16),
        "gq": two(p["gqa_q_g"][l]), "gk": two(p["gqa_k_g"][l]),
        "gmq": p["mla_q_g"][l][None, :], "gmkv": p["mla_kv_g"][l][None, :],
        "wq": wq, "wabs": wabs.astype(BF16), "wvt": wvt.astype(BF16), "krep": jnp.asarray(krep, BF16),
        "w_out": w_out,
        "w1": p["ffn_w1"][l].astype(BF16), "w3": p["ffn_w3"][l].astype(BF16), "w2": p["ffn_w2"][l].astype(BF16),
        "conv_w": p["hy_conv_w"][l], "conv_b": p["hy_conv_b"][l][None, :], "hy_bias": p["hy_bias"][l],
    }


def kernel(x, c, ctx, c_ctx, mod_w, mod_b, norm1_g, norm2_g, w_in, hy_conv_w, hy_conv_b, hy_filt_w1, hy_filt_b1, hy_filt_w2, hy_filt_b2, hy_filt_w3, hy_filt_freq, hy_bias, gqa_q_g, gqa_k_g, mla_q_g, mla_kv_g, mla_w_uq, mla_w_ukv, w_out, ffn_w1, ffn_w3, ffn_w2, final_g):
    p = dict(norm1_g=norm1_g, norm2_g=norm2_g, w_in=w_in, hy_conv_w=hy_conv_w, hy_conv_b=hy_conv_b,
             hy_bias=hy_bias, gqa_q_g=gqa_q_g, gqa_k_g=gqa_k_g, mla_q_g=mla_q_g, mla_kv_g=mla_kv_g,
             mla_w_uq=mla_w_uq, mla_w_ukv=mla_w_ukv, w_out=w_out, ffn_w1=ffn_w1, ffn_w3=ffn_w3, ffn_w2=ffn_w2)
    B, L, D = x.shape
    Lc = ctx.shape[1]
    depth = mod_w.shape[0]

    pad_rows = (-(B + 1)) % 8
    c_all = jnp.concatenate([c, c_ctx[None, :], jnp.zeros((pad_rows, D), F32)], axis=0)
    mod = _modulation(c_all, mod_w, mod_b).reshape(depth, -1, N_MOD, D)

    tabs_lat = _rope_tables(L, True)
    tabs_ctx = _rope_tables(Lc, False)
    final_g2 = final_g[None, :]

    x_lat, x_ctx = x, ctx
    for l in range(depth):
        last = l == depth - 1
        lw = _layer_weights(l, p)
        filt = (hy_filt_w1[l], hy_filt_b1[l], hy_filt_w2[l], hy_filt_b2[l], hy_filt_w3[l], hy_filt_freq[l])
        mod_lat = mod[l, :B]
        mod_ctx = jnp.broadcast_to(mod[l, B][None], (B, N_MOD, D))

        ut_c, q_c, k_c, vt_c, qa_c, qp_c, km_c, vmt_c = _inproj(x_ctx, mod_ctx, lw, tabs_ctx)
        ut, q, k, vt, qa, qp, km, vmt = _inproj(x_lat, mod_lat, lw, tabs_lat)

        if not last:
            yh_c = _longconv(ut_c, _hyena_filters(Lc, *filt), lw["hy_bias"])
            yg_c = _gqa_attention(q_c, [k_c], [vt_c])
            ym_c = _mla_attention(qa_c, qp_c, lw["wvt"], [km_c], [vmt_c])
            x_ctx_next = _outffn(x_ctx, yh_c, yg_c, ym_c, mod_ctx, lw, final_g2, False)

        yh = _longconv(ut, _hyena_filters(L, *filt), lw["hy_bias"])
        yg = _gqa_attention(q, [k, k_c], [vt, vt_c])
        ym = _mla_attention(qa, qp, lw["wvt"], [km, km_c], [vmt, vmt_c])
        x_lat = _outffn(x_lat, yh, yg, ym, mod_lat, lw, final_g2, last)
        if not last:
            x_ctx = x_ctx_next
    return x_lat
```

```python
import functools
import math

import jax
import jax.numpy as jnp
import numpy as np
from jax import lax
from jax.experimental import pallas as pl
from jax.experimental.pallas import tpu as pltpu

F32 = jnp.float32
BF16 = jnp.bfloat16

N_MOD = 6
GRID_W = 64
NORM_EPS = 1e-6
ROPE_THETA = 10000.0
HY_W = 256
HY_ORDER = 2
HY_BANDS = 16
HY_TARGET = 1e-2
HY_FAST_PCT = 0.3
HY_SLOW_PCT = 1.5
GQA_HEADS = 6
GQA_KV_HEADS = 2
GQA_HEAD_DIM = 64
MLA_HEADS = 6
MLA_NOPE_DIM = 64
MLA_ROPE_DIM = 32
MLA_V_DIM = 64
MLA_QK_DIM = MLA_NOPE_DIM + MLA_ROPE_DIM
IN_SPLITS = (768, 384, 128, 128, 256, 128, 32)
IN_OFF = tuple(int(v) for v in np.cumsum((0,) + IN_SPLITS))
IN_WIDTH_PAD = 1920

LANES = 128
SUBLANES = 8
BF16_ROWS = 16
MXU_TILE = 256
MLA_PE_LANES = -(-MLA_HEADS * MLA_ROPE_DIM // LANES) * LANES
VMEM_LIMIT = 56 * 1024 * 1024
CONV_BLK = MXU_TILE
CONV_CH = SUBLANES

TOKEN_TILE_IN = 1024
TOKEN_TILE_OUT = 512
QUERY_TILE = 128
KEY_CHUNK = MXU_TILE
FFN_HIDDEN_CHUNK = 6 * MXU_TILE
MOD_COL_TILE = 1536

LOG2E = 1.4426950408889634
HIGHEST = lax.Precision.HIGHEST


def _cparams(*sem):
    return pltpu.CompilerParams(dimension_semantics=sem, vmem_limit_bytes=VMEM_LIMIT)


def _const_spec(shape):
    zeros = (0,) * len(shape)
    return pl.BlockSpec(shape, lambda *_: zeros, pipeline_mode=pl.Buffered(1))


def _mod_kernel(c_ref, w_ref, b_ref, o_ref):
    c = c_ref[...]
    sc = c * jax.nn.sigmoid(c)
    o_ref[0] = jnp.dot(sc, w_ref[0], precision=HIGHEST, preferred_element_type=F32) + b_ref[0]


def _modulation(c_all, mod_w, mod_b):
    depth, d, n = mod_w.shape
    rows = c_all.shape[0]
    tn = MOD_COL_TILE
    return pl.pallas_call(
        _mod_kernel,
        grid=(depth, n // tn),
        in_specs=[
            pl.BlockSpec((rows, d), lambda l, j: (0, 0)),
            pl.BlockSpec((1, d, tn), lambda l, j: (l, 0, j)),
            pl.BlockSpec((1, 1, tn), lambda l, j: (l, 0, j)),
        ],
        out_specs=pl.BlockSpec((1, rows, tn), lambda l, j: (l, 0, j)),
        out_shape=jax.ShapeDtypeStruct((depth, rows, n), F32),
        compiler_params=_cparams("parallel", "parallel"),
        name="modulation",
    )(c_all, mod_w, mod_b.reshape(depth, 1, n))


def _filter_kernel(zt_ref, w1t_ref, b1_ref, w2t_ref, b2_ref, fr_ref, w3f_ref, w3b_ref,
                   trow_ref, sel_ref, delta_ref, o_ref, h_ref):
    @pl.when(pl.program_id(0) == 0)
    def _():
        fr = fr_ref[...]
        h = jnp.sin(fr * (jnp.dot(w1t_ref[...], zt_ref[...], precision=HIGHEST,
                                  preferred_element_type=F32) + b1_ref[...]))
        h_ref[...] = jnp.sin(fr * (jnp.dot(w2t_ref[...], h, precision=HIGHEST,
                                           preferred_element_type=F32) + b2_ref[...]))

    h = h_ref[...]
    h_hi = h.astype(BF16)
    h_lo = (h - h_hi.astype(F32)).astype(BF16)

    def dot3(w):
        w_hi = w.astype(BF16)
        w_lo = (w - w_hi.astype(F32)).astype(BF16)
        return (jnp.dot(w_hi, h_hi, preferred_element_type=F32)
                + (jnp.dot(w_hi, h_lo, preferred_element_type=F32) + jnp.dot(w_lo, h_hi, preferred_element_type=F32)))

    fwd = dot3(w3f_ref[...])
    bwd = dot3(w3b_ref[...])
    sel = sel_ref[...]
    k = jnp.where(sel > 0.0, fwd, jnp.where(sel < 0.0, bwd, 0.0))
    k = k * jnp.exp(-delta_ref[...] * trow_ref[...])
    nrm = jnp.sum(jnp.abs(k), axis=1, keepdims=True)
    o_ref[0] = k / nrm


def _filter_features(L):
    t = np.linspace(0.0, 1.0, L, dtype=np.float32)
    wpos = np.float32(2.0 * math.pi) * np.arange(L, dtype=np.float32) / np.float32(L)
    f = np.linspace(1e-4, HY_BANDS - 1, HY_BANDS, dtype=np.float32)
    ang = (f[None, :] * wpos[:, None]).astype(np.float64)
    z = np.concatenate([t[:, None], np.cos(ang), -np.sin(ang)], axis=1).astype(np.float32)
    n = np.arange(2 * L)
    lag = np.where(n <= L, np.minimum(n, L - 1), 2 * L - n)
    sel = np.where(n < L, 1.0, np.where(n > L, -1.0, 0.0)).astype(np.float32)
    zt = np.zeros((LANES, 2 * L), np.float32)
    zt[:z.shape[1]] = z[lag].T
    return zt, t[lag][None, :], sel[None, :]


def _hyena_filters(L, w1, b1, w2, b2, w3, freq):
    P = 2 * L
    zt, trow, sel = _filter_features(L)
    nf = w1.shape[1]
    w1t = jnp.pad(w1, ((0, LANES - w1.shape[0]), (0, 0))).T
    col = lambda a: a[:, None]
    deltas = np.abs(np.linspace(math.log(HY_TARGET) / HY_FAST_PCT, math.log(HY_TARGET) / HY_SLOW_PCT,
                                HY_W, dtype=np.float32))[:, None]
    halves = HY_W // LANES
    cb = 2 * halves
    whole = lambda shape: pl.BlockSpec(shape, lambda s: (0, 0))
    return pl.pallas_call(
        _filter_kernel,
        grid=(HY_ORDER * halves,),
        in_specs=[
            whole((LANES, P)), whole((nf, LANES)), whole((nf, 1)), whole((nf, nf)), whole((nf, 1)),
            whole((nf, 1)),
            pl.BlockSpec((LANES, nf), lambda s: ((s // halves) * cb + s % halves, 0)),
            pl.BlockSpec((LANES, nf), lambda s: ((s // halves) * cb + halves + s % halves, 0)),
            whole((1, P)), whole((1, P)),
            pl.BlockSpec((LANES, 1), lambda s: (s % halves, 0)),
        ],
        out_specs=pl.BlockSpec((1, LANES, P), lambda s: (s // halves, s % halves, 0)),
        out_shape=jax.ShapeDtypeStruct((HY_ORDER, HY_W, P), F32),
        scratch_shapes=[pltpu.VMEM((nf, P), F32)],
        compiler_params=_cparams("arbitrary"),
        name="hyena_filters",
    )(zt, w1t, col(b1), w2.T, col(b2), col(freq), w3.T, w3.T, trow, sel, deltas)


def _rope(t, c_ref, s1_ref, s2_ref, sh):
    return (t * c_ref[...] + pltpu.roll(t, LANES - sh, 1) * s1_ref[...]
            + pltpu.roll(t, sh, 1) * s2_ref[...])


def _inproj_kernel(x_ref, xp_ref, xn_ref, mod_ref, g1_ref, w_ref, cw_ref, cb_ref,
                   e2_ref, gq_ref, gk_ref, gmq_ref, gmkv_ref, wq_ref, wabs_ref, krep_ref,
                   cg_ref, s1g_ref, s2g_ref, cm_ref, s1m_ref, s2m_ref,
                   ut_ref, q_ref, k_ref, vt_ref, qabs_ref, qpe_ref, km_ref, vmt_ref, *, q_scale, qm_scale):
    shift = mod_ref[0, 0:1, :]
    scale = mod_ref[0, 1:2, :]

    def norm_mod(xx):
        hh = xx * lax.rsqrt(jnp.mean(xx * xx, axis=-1, keepdims=True) + NORM_EPS) * g1_ref[...]
        return (hh * (1.0 + scale) + shift).astype(BF16)

    u = jnp.dot(norm_mod(x_ref[0]), w_ref[...], preferred_element_type=F32)

    hy_w = IN_OFF[1]
    tm = u.shape[0]
    i, n_i = pl.program_id(0), pl.num_programs(0)
    edge = jnp.dot(norm_mod(jnp.concatenate([xp_ref[0], xn_ref[0]], axis=0)), w_ref[:, 0:hy_w],
                   preferred_element_type=F32)
    before = jnp.where(i > 0, edge[SUBLANES - 1:SUBLANES, :], 0.0)
    after = jnp.where(i < n_i - 1, edge[SUBLANES:SUBLANES + 1, :], 0.0)
    uh = u[:, 0:hy_w]
    row = lax.broadcasted_iota(jnp.int32, uh.shape, 0)
    prev = jnp.where(row == 0, before, pltpu.roll(uh, 1, 0))
    nxt = jnp.where(row == tm - 1, after, pltpu.roll(uh, tm - 1, 0))
    y = prev * cw_ref[0:1, :] + uh * cw_ref[1:2, :] + nxt * cw_ref[2:3, :] + cb_ref[...]
    for p in range(hy_w // HY_W):
        ut_ref[p, 0] = y[:, HY_W * p: HY_W * (p + 1)].T

    e2 = e2_ref[...]

    def head_norm(t, g):
        sq = t * t
        hi = sq.astype(BF16)
        lo = (sq - hi.astype(F32)).astype(BF16)
        ss = (jnp.dot(hi, e2, preferred_element_type=F32) + jnp.dot(lo, e2, preferred_element_type=F32))
        return t * lax.rsqrt(ss * (1.0 / GQA_HEAD_DIM) + NORM_EPS) * g

    for i in range(GQA_HEADS * GQA_HEAD_DIM // LANES):
        t = u[:, IN_OFF[1] + LANES * i: IN_OFF[1] + LANES * (i + 1)]
        t = _rope(head_norm(t, gq_ref[...]), cg_ref, s1g_ref, s2g_ref, GQA_HEAD_DIM // 4)
        q_ref[0, :, LANES * i: LANES * (i + 1)] = (t * q_scale).astype(BF16)
    t = u[:, IN_OFF[2]:IN_OFF[3]]
    k_ref[0] = _rope(head_norm(t, gk_ref[...]), cg_ref, s1g_ref, s2g_ref, GQA_HEAD_DIM // 4).astype(BF16)
    vt_ref[0] = u[:, IN_OFF[3]:IN_OFF[4]].T.astype(BF16)

    mq = u[:, IN_OFF[4]:IN_OFF[5]]
    mqn = mq * lax.rsqrt(jnp.mean(mq * mq, axis=-1, keepdims=True) + NORM_EPS) * gmq_ref[...]
    qm = jnp.dot(mqn.astype(BF16), wq_ref[...], preferred_element_type=F32)
    n_nope = MLA_HEADS * MLA_NOPE_DIM
    qabs = jnp.dot(qm[:, :n_nope].astype(BF16), wabs_ref[...], preferred_element_type=F32)
    qabs_ref[0] = (qabs * qm_scale).astype(BF16)
    for i in range(qpe_ref.shape[2] // LANES):
        t = _rope(qm[:, n_nope + LANES * i: n_nope + LANES * (i + 1)], cm_ref, s1m_ref, s2m_ref, MLA_ROPE_DIM // 4)
        qpe_ref[0, :, LANES * i: LANES * (i + 1)] = (t * qm_scale).astype(BF16)

    mkv = u[:, IN_OFF[5]:IN_OFF[6]]
    kvn = mkv * lax.rsqrt(jnp.mean(mkv * mkv, axis=-1, keepdims=True) + NORM_EPS) * gmkv_ref[...]
    kpe = _rope(u[:, IN_OFF[6]:IN_OFF[6] + LANES], cm_ref, s1m_ref, s2m_ref, MLA_ROPE_DIM // 4).astype(BF16)
    kpe4 = jnp.dot(kpe, krep_ref[...], preferred_element_type=F32).astype(BF16)
    km_ref[0] = jnp.concatenate([kvn.astype(BF16), kpe4], axis=1)
    vmt_ref[0] = kvn.T.astype(BF16)


def _inproj(x, mod, lw, tabs):
    B, L, D = x.shape
    tm = min(TOKEN_TILE_IN, L)
    tab_spec = pl.BlockSpec((tm, LANES), lambda i, b: (i, 0))
    tok = lambda w: pl.BlockSpec((1, tm, w), lambda i, b: (b, i, 0))
    kern = functools.partial(_inproj_kernel,
                             q_scale=GQA_HEAD_DIM ** -0.5 * LOG2E, qm_scale=MLA_QK_DIM ** -0.5 * LOG2E)
    consts = [lw["g1"], lw["w_in"], lw["conv_w"], lw["conv_b"], lw["e2"], lw["gq"], lw["gk"], lw["gmq"], lw["gmkv"],
              lw["wq"], lw["wabs"], lw["krep"]]
    latent = lw["wabs"].shape[1] // MLA_HEADS
    out_w = [(IN_SPLITS[1], BF16, False), (IN_SPLITS[2], BF16, False),
             (IN_SPLITS[3], BF16, True), (MLA_HEADS * latent, BF16, False), (MLA_PE_LANES, BF16, False),
             (latent + LANES, BF16, False), (latent, BF16, True)]
    tok_t = lambda w: pl.BlockSpec((1, w, tm), lambda i, b: (b, 0, i))
    nparts = IN_SPLITS[0] // HY_W
    per = tm // SUBLANES
    x_before = pl.BlockSpec((1, SUBLANES, D), lambda i, b: (b, jnp.maximum(i * per - 1, 0), 0))
    x_after = pl.BlockSpec((1, SUBLANES, D), lambda i, b: (b, jnp.minimum((i + 1) * per, L // SUBLANES - 1), 0))
    return pl.pallas_call(
        kern,
        grid=(L // tm, B),
        in_specs=[tok(D), x_before, x_after, pl.BlockSpec((1, N_MOD, D), lambda i, b: (b, 0, 0))]
        + [_const_spec(a.shape) for a in consts] + [tab_spec] * len(tabs),
        out_specs=[pl.BlockSpec((nparts, 1, HY_W, tm), lambda i, b: (0, b, 0, i))]
        + [tok_t(w) if tr else tok(w) for w, _, tr in out_w],
        out_shape=[jax.ShapeDtypeStruct((nparts, B, HY_W, L), F32)]
        + [jax.ShapeDtypeStruct((B, w, L) if tr else (B, L, w), dt) for w, dt, tr in out_w],
        compiler_params=_cparams("parallel", "parallel"),
        name="inproj",
    )(x, x, x, mod, *consts, *tabs)


def _longconv_kernel(bias_ref, kf_ref, u_ref, o_ref, s0_ref, s1_ref, acc_ref, x_ref, y_ref, *, nblk):
    g = pl.program_id(0)
    B, CH = u_ref.shape[1], u_ref.shape[2]
    P = s0_ref.shape[1]
    for p in range(3):
        x_ref[p] = jnp.swapaxes(u_ref[p], 0, 1)

    def to_rows(z):
        return jnp.concatenate([z[:, j * CONV_BLK:(j + 1) * CONV_BLK] for j in range(nblk)], axis=0)

    def conv(zin, o, c, s_ref):
        krow = kf_ref[o, pl.ds(c, 1), :]
        s_ref[...] = pltpu.roll(jnp.broadcast_to(krow, (CONV_BLK, P)), 0, 1,
                                stride=1, stride_axis=0).astype(BF16)
        zb = zin.astype(BF16)
        acc_ref[...] = jnp.zeros_like(acc_ref)
        for d in range(-(nblk - 1), nblk):
            j0 = max(0, -d)
            n = nblk - abs(d)
            i0 = j0 + d
            col = (d * CONV_BLK) % P
            acc_ref[B * i0: B * (i0 + n), :] += jnp.dot(
                zb[B * j0: B * (j0 + n), :], s_ref[:, col: col + CONV_BLK],
                preferred_element_type=F32)
        return acc_ref[...] + zin * bias_ref[o, g * CH + c]

    def channel(c, carry):
        z1 = to_rows(x_ref[1, c]) * conv(to_rows(x_ref[0, c]), 0, c, s0_ref)
        y = to_rows(x_ref[2, c]) * conv(z1, 1, c, s1_ref)
        y_ref[c] = jnp.concatenate([y[B * i: B * (i + 1)] for i in range(nblk)], axis=1)
        return carry

    lax.fori_loop(0, CH, channel, 0, unroll=True)
    o_ref[...] = jnp.swapaxes(y_ref[...], 0, 1)


def _longconv(ut, kf, hy_bias):
    _, B, C, L = ut.shape
    P = kf.shape[-1]
    nblk = L // CONV_BLK
    return pl.pallas_call(
        functools.partial(_longconv_kernel, nblk=nblk),
        grid=(C // CONV_CH,),
        in_specs=[
            pl.BlockSpec(memory_space=pltpu.SMEM),
            pl.BlockSpec((HY_ORDER, CONV_CH, P), lambda g: (0, g, 0)),
            pl.BlockSpec((3, B, CONV_CH, L), lambda g: (0, 0, g, 0)),
        ],
        out_specs=pl.BlockSpec((B, CONV_CH, L), lambda g: (0, g, 0)),
        out_shape=jax.ShapeDtypeStruct((B, C, L), F32),
        scratch_shapes=[pltpu.VMEM((CONV_BLK, P), BF16), pltpu.VMEM((CONV_BLK, P), BF16),
                        pltpu.VMEM((nblk * B, CONV_BLK), F32),
                        pltpu.VMEM((3, CONV_CH, B, L), F32), pltpu.VMEM((CONV_CH, B, L), F32)],
        compiler_params=_cparams("parallel"),
        name="hyena_longconv",
    )(hy_bias, kf, ut)


def _attn_units_step(qs, k_refs, vt_refs, s_new, s_old, m_new, m_old, koffs, voffs, vrows, tk):
    nu = len(qs)
    M = qs[0].shape[0]
    dn = (((1,), (1,)), ((), ()))
    row = 0
    for k_ref in k_refs:
        n = k_ref.shape[1]
        for u in range(nu):
            s_new[u, row:row + n, :] = lax.dot_general(k_ref[0, :, koffs[u]:koffs[u] + qs[u].shape[1]], qs[u], dn,
                                                       preferred_element_type=F32)
        row += n
    mo = [m_old[u, 0:1, :] for u in range(nu)]
    mx = [jnp.full((SUBLANES, M), -jnp.inf, F32) for _ in range(nu)]
    ones = jnp.ones((BF16_ROWS, tk), BF16)
    acc = [jnp.zeros((vrows + BF16_ROWS, M), F32) for _ in range(nu)]
    row = 0
    for vt_ref in vt_refs:
        for c in range(vt_ref.shape[2] // tk):
            rows = slice(row, row + tk)
            row += tk
            for u in range(nu):
                p = jnp.exp2(s_old[u, rows, :] - mo[u]).astype(BF16)
                vt = jnp.concatenate([vt_ref[0, voffs[u]:voffs[u] + vrows, c * tk:(c + 1) * tk], ones], axis=0)
                acc[u] = acc[u] + jnp.dot(vt, p, preferred_element_type=F32)
                sn = s_new[u, rows, :]
                for r in range(tk // SUBLANES):
                    mx[u] = jnp.maximum(mx[u], sn[SUBLANES * r: SUBLANES * (r + 1), :])
    for u in range(nu):
        m_new[u] = jnp.broadcast_to(jnp.max(mx[u], axis=0, keepdims=True), (SUBLANES, M))
    return [a[:vrows, :] for a in acc], [a[vrows:vrows + 1, :] for a in acc]


def _attn_pipeline(step, s0_ref, s1_ref, m0_ref, m1_ref, e0_ref, e1_ref):
    g = pl.program_id(0)

    @pl.when(g == 0)
    def _():
        s1_ref[...] = jnp.zeros_like(s1_ref)
        m1_ref[...] = jnp.zeros_like(m1_ref)
        e1_ref[...] = jnp.ones_like(e1_ref)

    @pl.when(g % 2 == 0)
    def _():
        step(s0_ref, s1_ref, m0_ref, m1_ref, e0_ref, e1_ref)

    @pl.when(g % 2 == 1)
    def _():
        step(s1_ref, s0_ref, m1_ref, m0_ref, e1_ref, e0_ref)


def _attn_tile_maps(ntiles):
    scored = lambda g: jnp.minimum(g, ntiles - 1)
    valued = lambda g: jnp.clip(g - 1, 0, ntiles - 1)
    written = lambda g: jnp.maximum(g - 2, 0)
    return scored, valued, written


def _park(e_ref, acc, den):
    rows = acc.shape[0]
    e_ref[0:rows, :] = acc
    e_ref[rows:rows + SUBLANES, :] = jnp.broadcast_to(den, (SUBLANES, den.shape[1]))


def _gqa_kernel(*refs, nseg, tk):
    q_ref, k_refs, v_refs = refs[0], refs[1:1 + nseg], refs[1 + nseg:1 + 2 * nseg]
    o_ref, s0_ref, s1_ref, m0_ref, m1_ref, e0_ref, e1_ref = refs[1 + 2 * nseg:]
    tq = q_ref.shape[1]
    ntile = q_ref.shape[2] // LANES
    vrows = GQA_KV_HEADS * GQA_HEAD_DIM

    def step(s_new, s_old, m_new, m_old, e_new, e_old):
        o = e_old[0:vrows, :] * (1.0 / e_old[vrows:vrows + 1, :])
        top = lax.broadcasted_iota(jnp.int32, (vrows, tq), 0) < GQA_HEAD_DIM
        for t in range(ntile):
            oa = o[:, (2 * t) * tq:(2 * t + 1) * tq]
            ob = o[:, (2 * t + 1) * tq:(2 * t + 2) * tq]
            o_ref[0, :, LANES * t: LANES * (t + 1)] = jnp.where(top, oa, ob).T.astype(BF16)

        low = lax.broadcasted_iota(jnp.int32, (tq, LANES), 1) < GQA_HEAD_DIM
        zero = jnp.zeros((tq, LANES), BF16)
        qs = []
        for t in range(ntile):
            qt = q_ref[0, :, LANES * t: LANES * (t + 1)]
            qs.append(jnp.where(low, qt, zero))
            qs.append(jnp.where(low, zero, qt))
        (acc,), (den,) = _attn_units_step([jnp.concatenate(qs, axis=0)], k_refs, v_refs, s_new, s_old,
                                          m_new, m_old, (0,), (0,), vrows, tk)
        _park(e_new, acc, den)

    _attn_pipeline(step, s0_ref, s1_ref, m0_ref, m1_ref, e0_ref, e1_ref)


def _gqa_attention(q, ks, vts):
    B, L, W = q.shape
    T = sum(k.shape[1] for k in ks)
    tq = min(QUERY_TILE, L)
    nq = L // tq
    M = GQA_HEADS * tq
    scored, valued, written = _attn_tile_maps(B * nq)
    erows = GQA_KV_HEADS * GQA_HEAD_DIM + SUBLANES
    return pl.pallas_call(
        functools.partial(_gqa_kernel, nseg=len(ks), tk=KEY_CHUNK),
        grid=(B * nq + 2,),
        in_specs=[pl.BlockSpec((1, tq, W), lambda g: (scored(g) // nq, scored(g) % nq, 0))]
        + [pl.BlockSpec((1, k.shape[1], LANES), lambda g: (scored(g) // nq, 0, 0)) for k in ks]
        + [pl.BlockSpec((1, LANES, v.shape[2]), lambda g: (valued(g) // nq, 0, 0)) for v in vts],
        out_specs=pl.BlockSpec((1, tq, W), lambda g: (written(g) // nq, written(g) % nq, 0)),
        out_shape=jax.ShapeDtypeStruct((B, L, W), BF16),
        scratch_shapes=[pltpu.VMEM((1, T, M), F32), pltpu.VMEM((1, T, M), F32),
                        pltpu.VMEM((1, SUBLANES, M), F32), pltpu.VMEM((1, SUBLANES, M), F32),
                        pltpu.VMEM((erows, M), F32), pltpu.VMEM((erows, M), F32)],
        compiler_params=_cparams("arbitrary"),
        name="gqa_attention",
    )(q, *ks, *vts)


def _mla_kernel(*refs, nseg, tk):
    qabs_ref, qpe_ref, wvt_ref = refs[0:3]
    k_refs, v_refs = refs[3:3 + nseg], refs[3 + nseg:3 + 2 * nseg]
    o_ref, s0_ref, s1_ref, m0_ref, m1_ref, e0_ref, e1_ref = refs[3 + 2 * nseg:]
    tq = qabs_ref.shape[1]
    latent = v_refs[0].shape[1]
    slots = LANES // MLA_ROPE_DIM

    def step(s_new, s_old, m_new, m_old, e_new, e_old):
        olat = (e_old[0:latent, :] * (1.0 / e_old[latent:latent + 1, :])).astype(BF16)
        outs = [jnp.dot(wvt_ref[MLA_V_DIM * h: MLA_V_DIM * (h + 1), :], olat[:, tq * h: tq * (h + 1)],
                        preferred_element_type=F32) for h in range(MLA_HEADS)]
        for t in range(MLA_HEADS // 2):
            o_ref[0, :, LANES * t: LANES * (t + 1)] = jnp.concatenate(outs[2 * t: 2 * t + 2], axis=0).T.astype(BF16)

        lane = lax.broadcasted_iota(jnp.int32, (tq, LANES), 1)
        zero = jnp.zeros((tq, LANES), BF16)
        qs = []
        for h in range(MLA_HEADS):
            pe = qpe_ref[0, :, LANES * (h // slots): LANES * (h // slots + 1)]
            mine = (lane >= MLA_ROPE_DIM * (h % slots)) & (lane < MLA_ROPE_DIM * (h % slots + 1))
            qs.append(jnp.concatenate([qabs_ref[0, :, latent * h: latent * (h + 1)], jnp.where(mine, pe, zero)], axis=1))
        (acc,), (den,) = _attn_units_step([jnp.concatenate(qs, axis=0)], k_refs, v_refs, s_new, s_old,
                                          m_new, m_old, (0,), (0,), latent, tk)
        _park(e_new, acc, den)

    _attn_pipeline(step, s0_ref, s1_ref, m0_ref, m1_ref, e0_ref, e1_ref)


def _mla_attention(qabs, qpe, wvt, ks, vts):
    B, L, _ = qabs.shape
    T = sum(k.shape[1] for k in ks)
    tq = min(QUERY_TILE, L)
    nq = L // tq
    M = MLA_HEADS * tq
    scored, valued, written = _attn_tile_maps(B * nq)
    W = MLA_HEADS * MLA_V_DIM
    erows = vts[0].shape[1] + SUBLANES
    return pl.pallas_call(
        functools.partial(_mla_kernel, nseg=len(ks), tk=KEY_CHUNK),
        grid=(B * nq + 2,),
        in_specs=[pl.BlockSpec((1, tq, qabs.shape[2]), lambda g: (scored(g) // nq, scored(g) % nq, 0)),
                  pl.BlockSpec((1, tq, qpe.shape[2]), lambda g: (scored(g) // nq, scored(g) % nq, 0)),
                  _const_spec(wvt.shape)]
        + [pl.BlockSpec((1,) + k.shape[1:], lambda g: (scored(g) // nq, 0, 0)) for k in ks]
        + [pl.BlockSpec((1,) + v.shape[1:], lambda g: (valued(g) // nq, 0, 0)) for v in vts],
        out_specs=pl.BlockSpec((1, tq, W), lambda g: (written(g) // nq, written(g) % nq, 0)),
        out_shape=jax.ShapeDtypeStruct((B, L, W), BF16),
        scratch_shapes=[pltpu.VMEM((1, T, M), F32), pltpu.VMEM((1, T, M), F32),
                        pltpu.VMEM((1, SUBLANES, M), F32), pltpu.VMEM((1, SUBLANES, M), F32),
                        pltpu.VMEM((erows, M), F32), pltpu.VMEM((erows, M), F32)],
        compiler_params=_cparams("arbitrary"),
        name="mla_attention",
    )(qabs, qpe, wvt, *ks, *vts)


def _outffn_kernel(x_ref, yh_ref, yg_ref, ym_ref, mod_ref, g2_ref, wo_ref, w1_ref, w3_ref, w2_ref,
                   gf_ref, o_ref, *, hchunk, final):
    x = x_ref[0]
    mix = jnp.concatenate([yh_ref[0].T.astype(BF16), yg_ref[0], ym_ref[0]], axis=1)
    y = jnp.dot(mix, wo_ref[...], preferred_element_type=F32)
    x1 = x + mod_ref[0, 2:3, :] * y
    h = x1 * lax.rsqrt(jnp.mean(x1 * x1, axis=-1, keepdims=True) + NORM_EPS) * g2_ref[...]
    h = (h * (1.0 + mod_ref[0, 4:5, :]) + mod_ref[0, 3:4, :]).astype(BF16)
    f = jnp.zeros_like(x)
    hidden = w1_ref.shape[1]
    for lo in range(0, hidden, hchunk):
        cols = slice(lo, min(lo + hchunk, hidden))
        a = jnp.dot(h, w1_ref[:, cols], preferred_element_type=F32)
        b = jnp.dot(h, w3_ref[:, cols], preferred_element_type=F32)
        g = (a * jax.nn.sigmoid(a) * b).astype(BF16)
        f = f + jnp.dot(g, w2_ref[cols, :], preferred_element_type=F32)
    x2 = x1 + mod_ref[0, 5:6, :] * f
    if final:
        x2 = x2 * lax.rsqrt(jnp.mean(x2 * x2, axis=-1, keepdims=True) + NORM_EPS) * gf_ref[...]
    o_ref[0] = x2


def _outffn(x, yh_t, yg, ym, mod, lw, final_g, final):
    B, L, D = x.shape
    tm = min(TOKEN_TILE_OUT, L)
    hidden = lw["w1"].shape[1]
    consts = [lw["g2"], lw["w_out"], lw["w1"], lw["w3"], lw["w2"], final_g]
    kern = functools.partial(_outffn_kernel, hchunk=FFN_HIDDEN_CHUNK, final=final)
    tok = lambda w: pl.BlockSpec((1, tm, w), lambda b, i: (b, i, 0))
    return pl.pallas_call(
        kern,
        grid=(B, L // tm),
        in_specs=[tok(D), pl.BlockSpec((1, HY_W, tm), lambda b, i: (b, 0, i)),
                  tok(yg.shape[2]), tok(ym.shape[2]),
                  pl.BlockSpec((1, N_MOD, D), lambda b, i: (b, 0, 0))]
        + [_const_spec(a.shape) for a in consts],
        out_specs=tok(D),
        out_shape=jax.ShapeDtypeStruct((B, L, D), F32),
        compiler_params=_cparams("parallel", "parallel"),
        name="outffn",
    )(x, yh_t, yg, ym, mod, *consts)


def _rope_tables(L, with_pos):
    lane = np.arange(LANES)

    def pattern(dim, off, width):
        loc = (lane - off) % dim
        half = dim // 2
        active = (lane >= off) & (lane < off + width)
        use_col = loc >= half
        fi = (loc % half) % (half // 2)
        first = (loc % half) < (half // 2)
        return active, use_col, fi, first, half

    def tables(dim, off, width):
        active, use_col, fi, first, half = pattern(dim, off, width)
        if not with_pos:
            one = jnp.ones((L, LANES), F32)
            zero = jnp.zeros((L, LANES), F32)
            return [one, zero, zero]
        row = jnp.repeat(jnp.arange(L // GRID_W, dtype=jnp.int32), GRID_W).astype(F32)
        col = jnp.tile(jnp.arange(GRID_W, dtype=jnp.int32), L // GRID_W).astype(F32)
        inv = ROPE_THETA ** (-jnp.arange(0, half, 2, dtype=F32) / half)
        pos = jnp.where(jnp.asarray(use_col)[None, :], col[:, None], row[:, None])
        ang = pos * inv[jnp.asarray(fi)][None, :]
        act = jnp.asarray(active)[None, :]
        fst = jnp.asarray(first)[None, :]
        cos = jnp.where(act, jnp.cos(ang), 1.0)
        sin = jnp.sin(ang)
        s1 = jnp.where(act & fst, -sin, 0.0)
        s2 = jnp.where(act & ~fst, sin, 0.0)
        return [cos, s1, s2]

    return tables(GQA_HEAD_DIM, 0, LANES) + tables(MLA_ROPE_DIM, 0, LANES)


def _gqa_tile_order():
    per = GQA_HEADS // GQA_KV_HEADS
    return [h for t in range(per) for h in (t, per + t)]


def _layer_weights(l, p):
    hd = GQA_HEAD_DIM
    order = _gqa_tile_order()
    qperm = np.concatenate([np.arange(hd) + hd * h for h in order])
    w_in = p["w_in"][l]
    w_in = jnp.concatenate([w_in[:, :IN_OFF[1]], w_in[:, IN_OFF[1]:IN_OFF[2]][:, qperm], w_in[:, IN_OFF[2]:]], axis=1)
    w_in = jnp.pad(w_in, ((0, 0), (0, IN_WIDTH_PAD - w_in.shape[1]))).astype(BF16)

    wq = p["mla_w_uq"][l].reshape(-1, MLA_HEADS, MLA_QK_DIM)
    rank_q = wq.shape[0]
    wq_pe = wq[:, :, MLA_NOPE_DIM:].reshape(rank_q, MLA_HEADS * MLA_ROPE_DIM)
    wq = jnp.concatenate([wq[:, :, :MLA_NOPE_DIM].reshape(rank_q, MLA_HEADS * MLA_NOPE_DIM),
                          jnp.pad(wq_pe, ((0, 0), (0, MLA_PE_LANES - wq_pe.shape[1])))], axis=1).astype(BF16)

    wkv = p["mla_w_ukv"][l].reshape(-1, MLA_HEADS, MLA_NOPE_DIM + MLA_V_DIM)
    latent = wkv.shape[0]
    wk_t = jnp.transpose(wkv[:, :, :MLA_NOPE_DIM], (1, 2, 0))
    eye = jnp.asarray(np.eye(MLA_HEADS, dtype=np.float32))
    wabs = (wk_t[:, :, None, :] * eye[:, None, :, None]).reshape(MLA_HEADS * MLA_NOPE_DIM, MLA_HEADS * latent)
    wvt = jnp.transpose(wkv[:, :, MLA_NOPE_DIM:], (1, 2, 0)).reshape(MLA_HEADS * MLA_V_DIM, latent)
    krep = np.zeros((LANES, LANES), np.float32)
    for s in range(LANES // MLA_ROPE_DIM):
        krep[np.arange(MLA_ROPE_DIM), s * MLA_ROPE_DIM + np.arange(MLA_ROPE_DIM)] = 1.0

    w_out = p["w_out"][l]
    g0 = HY_W
    g1 = g0 + GQA_HEADS * hd
    w_out = jnp.concatenate([w_out[:g0], w_out[g0:g1][qperm], w_out[g1:]], axis=0).astype(BF16)

    e2 = np.kron(np.eye(LANES // hd, dtype=np.float32), np.ones((hd, hd), np.float32))
    two = lambda g: jnp.tile(g, LANES // hd)[None, :]
    return {
        "g1": p["norm1_g"][l][None, :], "g2": p["norm2_g"][l][None, :],
        "w_in": w_in, "e2": jnp.asarray(e2, BF16),
        "gq": two(p["gqa_q_g"][l]), "gk": two(p["gqa_k_g"][l]),
        "gmq": p["mla_q_g"][l][None, :], "gmkv": p["mla_kv_g"][l][None, :],
        "wq": wq, "wabs": wabs.astype(BF16), "wvt": wvt.astype(BF16), "krep": jnp.asarray(krep, BF16),
        "w_out": w_out,
        "w1": p["ffn_w1"][l].astype(BF16), "w3": p["ffn_w3"][l].astype(BF16), "w2": p["ffn_w2"][l].astype(BF16),
        "conv_w": p["hy_conv_w"][l], "conv_b": p["hy_conv_b"][l][None, :], "hy_bias": p["hy_bias"][l],
    }


def kernel(x, c, ctx, c_ctx, mod_w, mod_b, norm1_g, norm2_g, w_in, hy_conv_w, hy_conv_b, hy_filt_w1, hy_filt_b1, hy_filt_w2, hy_filt_b2, hy_filt_w3, hy_filt_freq, hy_bias, gqa_q_g, gqa_k_g, mla_q_g, mla_kv_g, mla_w_uq, mla_w_ukv, w_out, ffn_w1, ffn_w3, ffn_w2, final_g):
    p = dict(norm1_g=norm1_g, norm2_g=norm2_g, w_in=w_in, hy_conv_w=hy_conv_w, hy_conv_b=hy_conv_b,
             hy_bias=hy_bias, gqa_q_g=gqa_q_g, gqa_k_g=gqa_k_g, mla_q_g=mla_q_g, mla_kv_g=mla_kv_g,
             mla_w_uq=mla_w_uq, mla_w_ukv=mla_w_ukv, w_out=w_out, ffn_w1=ffn_w1, ffn_w3=ffn_w3, ffn_w2=ffn_w2)
    B, L, D = x.shape
    Lc = ctx.shape[1]
    depth = mod_w.shape[0]

    pad_rows = (-(B + 1)) % 8
    c_all = jnp.concatenate([c, c_ctx[None, :], jnp.zeros((pad_rows, D), F32)], axis=0)
    mod = _modulation(c_all, mod_w, mod_b).reshape(depth, -1, N_MOD, D)

    tabs_lat = _rope_tables(L, True)
    tabs_ctx = _rope_tables(Lc, False)
    final_g2 = final_g[None, :]

    x_lat, x_ctx = x, ctx
    for l in range(depth):
        last = l == depth - 1
        lw = _layer_weights(l, p)
        filt = (hy_filt_w1[l], hy_filt_b1[l], hy_filt_w2[l], hy_filt_b2[l], hy_filt_w3[l], hy_filt_freq[l])
        mod_lat = mod[l, :B]
        mod_ctx = jnp.broadcast_to(mod[l, B][None], (B, N_MOD, D))

        ut_c, q_c, k_c, vt_c, qa_c, qp_c, km_c, vmt_c = _inproj(x_ctx, mod_ctx, lw, tabs_ctx)
        ut, q, k, vt, qa, qp, km, vmt = _inproj(x_lat, mod_lat, lw, tabs_lat)

        if not last:
            yh_c = _longconv(ut_c, _hyena_filters(Lc, *filt), lw["hy_bias"])
            yg_c = _gqa_attention(q_c, [k_c], [vt_c])
            ym_c = _mla_attention(qa_c, qp_c, lw["wvt"], [km_c], [vmt_c])
            x_ctx_next = _outffn(x_ctx, yh_c, yg_c, ym_c, mod_ctx, lw, final_g2, False)

        yh = _longconv(ut, _hyena_filters(L, *filt), lw["hy_bias"])
        yg = _gqa_attention(q, [k, k_c], [vt, vt_c])
        ym = _mla_attention(qa, qp, lw["wvt"], [km, km_c], [vmt, vmt_c])
        x_lat = _outffn(x_lat, yh, yg, ym, mod_lat, lw, final_g2, last)
        if not last:
            x_ctx = x_ctx_next
    return x_lat
```

```python
import functools
import math

import jax
import jax.numpy as jnp
import numpy as np
from jax import lax
from jax.experimental import pallas as pl
from jax.experimental.pallas import tpu as pltpu

F32 = jnp.float32
BF16 = jnp.bfloat16

N_MOD = 6
GRID_W = 64
NORM_EPS = 1e-6
ROPE_THETA = 10000.0
HY_W = 256
HY_ORDER = 2
HY_BANDS = 16
HY_TARGET = 1e-2
HY_FAST_PCT = 0.3
HY_SLOW_PCT = 1.5
GQA_HEADS = 6
GQA_KV_HEADS = 2
GQA_HEAD_DIM = 64
MLA_HEADS = 6
MLA_NOPE_DIM = 64
MLA_ROPE_DIM = 32
MLA_V_DIM = 64
MLA_QK_DIM = MLA_NOPE_DIM + MLA_ROPE_DIM
IN_SPLITS = (768, 384, 128, 128, 256, 128, 32)
IN_OFF = tuple(int(v) for v in np.cumsum((0,) + IN_SPLITS))
IN_WIDTH_PAD = 1920

LANES = 128
SUBLANES = 8
BF16_ROWS = 16
MXU_TILE = 256
MLA_PE_LANES = -(-MLA_HEADS * MLA_ROPE_DIM // LANES) * LANES
VMEM_LIMIT = 56 * 1024 * 1024
CONV_BLK = MXU_TILE
CONV_CH = SUBLANES

TOKEN_TILE_IN = 1024
TOKEN_TILE_OUT = 512
QUERY_TILE = 128
KEY_CHUNK = MXU_TILE
FFN_HIDDEN_CHUNK = 6 * MXU_TILE
MOD_COL_TILE = 1536

LOG2E = 1.4426950408889634
HIGHEST = lax.Precision.HIGHEST


def _cparams(*sem):
    return pltpu.CompilerParams(dimension_semantics=sem, vmem_limit_bytes=VMEM_LIMIT)


def _const_spec(shape):
    zeros = (0,) * len(shape)
    return pl.BlockSpec(shape, lambda *_: zeros, pipeline_mode=pl.Buffered(1))


def _mod_kernel(c_ref, w_ref, b_ref, o_ref):
    c = c_ref[...]
    sc = c * jax.nn.sigmoid(c)
    w = w_ref[0]
    sc_hi = sc.astype(BF16)
    sc_lo = (sc - sc_hi.astype(F32)).astype(BF16)
    w_hi = w.astype(BF16)
    w_lo = (w - w_hi.astype(F32)).astype(BF16)
    o_ref[0] = (jnp.dot(sc_hi, w_hi, preferred_element_type=F32)
                + (jnp.dot(sc_hi, w_lo, preferred_element_type=F32) + jnp.dot(sc_lo, w_hi, preferred_element_type=F32))
                + b_ref[0])


def _modulation(c_all, mod_w, mod_b):
    depth, d, n = mod_w.shape
    rows = c_all.shape[0]
    tn = MOD_COL_TILE
    return pl.pallas_call(
        _mod_kernel,
        grid=(depth, n // tn),
        in_specs=[
            pl.BlockSpec((rows, d), lambda l, j: (0, 0)),
            pl.BlockSpec((1, d, tn), lambda l, j: (l, 0, j)),
            pl.BlockSpec((1, 1, tn), lambda l, j: (l, 0, j)),
        ],
        out_specs=pl.BlockSpec((1, rows, tn), lambda l, j: (l, 0, j)),
        out_shape=jax.ShapeDtypeStruct((depth, rows, n), F32),
        compiler_params=_cparams("parallel", "parallel"),
        name="modulation",
    )(c_all, mod_w, mod_b.reshape(depth, 1, n))


def _filter_kernel(zt_ref, w1t_ref, b1_ref, w2t_ref, b2_ref, fr_ref, w3f_ref, w3b_ref,
                   trow_ref, sel_ref, delta_ref, o_ref, h_ref):
    @pl.when(pl.program_id(0) == 0)
    def _():
        fr = fr_ref[...]
        h = jnp.sin(fr * (jnp.dot(w1t_ref[...], zt_ref[...], precision=HIGHEST,
                                  preferred_element_type=F32) + b1_ref[...]))
        h_ref[...] = jnp.sin(fr * (jnp.dot(w2t_ref[...], h, precision=HIGHEST,
                                           preferred_element_type=F32) + b2_ref[...]))

    h = h_ref[...]
    h_hi = h.astype(BF16)
    h_lo = (h - h_hi.astype(F32)).astype(BF16)

    def dot3(w):
        w_hi = w.astype(BF16)
        w_lo = (w - w_hi.astype(F32)).astype(BF16)
        return (jnp.dot(w_hi, h_hi, preferred_element_type=F32)
                + (jnp.dot(w_hi, h_lo, preferred_element_type=F32) + jnp.dot(w_lo, h_hi, preferred_element_type=F32)))

    fwd = dot3(w3f_ref[...])
    bwd = dot3(w3b_ref[...])
    sel = sel_ref[...]
    k = jnp.where(sel > 0.0, fwd, jnp.where(sel < 0.0, bwd, 0.0))
    k = k * jnp.exp(-delta_ref[...] * trow_ref[...])
    nrm = jnp.sum(jnp.abs(k), axis=1, keepdims=True)
    o_ref[0] = k / nrm


def _filter_features(L):
    t = np.linspace(0.0, 1.0, L, dtype=np.float32)
    wpos = np.float32(2.0 * math.pi) * np.arange(L, dtype=np.float32) / np.float32(L)
    f = np.linspace(1e-4, HY_BANDS - 1, HY_BANDS, dtype=np.float32)
    ang = (f[None, :] * wpos[:, None]).astype(np.float64)
    z = np.concatenate([t[:, None], np.cos(ang), -np.sin(ang)], axis=1).astype(np.float32)
    n = np.arange(2 * L)
    lag = np.where(n <= L, np.minimum(n, L - 1), 2 * L - n)
    sel = np.where(n < L, 1.0, np.where(n > L, -1.0, 0.0)).astype(np.float32)
    zt = np.zeros((LANES, 2 * L), np.float32)
    zt[:z.shape[1]] = z[lag].T
    return zt, t[lag][None, :], sel[None, :]


def _hyena_filters(L, w1, b1, w2, b2, w3, freq):
    P = 2 * L
    zt, trow, sel = _filter_features(L)
    nf = w1.shape[1]
    w1t = jnp.pad(w1, ((0, LANES - w1.shape[0]), (0, 0))).T
    col = lambda a: a[:, None]
    deltas = np.abs(np.linspace(math.log(HY_TARGET) / HY_FAST_PCT, math.log(HY_TARGET) / HY_SLOW_PCT,
                                HY_W, dtype=np.float32))[:, None]
    halves = HY_W // LANES
    cb = 2 * halves
    whole = lambda shape: pl.BlockSpec(shape, lambda s: (0, 0))
    return pl.pallas_call(
        _filter_kernel,
        grid=(HY_ORDER * halves,),
        in_specs=[
            whole((LANES, P)), whole((nf, LANES)), whole((nf, 1)), whole((nf, nf)), whole((nf, 1)),
            whole((nf, 1)),
            pl.BlockSpec((LANES, nf), lambda s: ((s // halves) * cb + s % halves, 0)),
            pl.BlockSpec((LANES, nf), lambda s: ((s // halves) * cb + halves + s % halves, 0)),
            whole((1, P)), whole((1, P)),
            pl.BlockSpec((LANES, 1), lambda s: (s % halves, 0)),
        ],
        out_specs=pl.BlockSpec((1, LANES, P), lambda s: (s // halves, s % halves, 0)),
        out_shape=jax.ShapeDtypeStruct((HY_ORDER, HY_W, P), F32),
        scratch_shapes=[pltpu.VMEM((nf, P), F32)],
        compiler_params=_cparams("arbitrary"),
        name="hyena_filters",
    )(zt, w1t, col(b1), w2.T, col(b2), col(freq), w3.T, w3.T, trow, sel, deltas)


def _rope(t, c_ref, s1_ref, s2_ref, sh):
    return (t * c_ref[...] + pltpu.roll(t, LANES - sh, 1) * s1_ref[...]
            + pltpu.roll(t, sh, 1) * s2_ref[...])


def _inproj_kernel(x_ref, xp_ref, xn_ref, mod_ref, g1_ref, w_ref, cw_ref, cb_ref,
                   e2_ref, gq_ref, gk_ref, gmq_ref, gmkv_ref, wq_ref, wabs_ref, krep_ref,
                   cg_ref, s1g_ref, s2g_ref, cm_ref, s1m_ref, s2m_ref,
                   ut_ref, q_ref, k_ref, vt_ref, qabs_ref, qpe_ref, km_ref, vmt_ref, *, q_scale, qm_scale):
    shift = mod_ref[0, 0:1, :]
    scale = mod_ref[0, 1:2, :]

    def norm_mod(xx):
        hh = xx * lax.rsqrt(jnp.mean(xx * xx, axis=-1, keepdims=True) + NORM_EPS) * g1_ref[...]
        return (hh * (1.0 + scale) + shift).astype(BF16)

    u = jnp.dot(norm_mod(x_ref[0]), w_ref[...], preferred_element_type=F32)

    hy_w = IN_OFF[1]
    tm = u.shape[0]
    i, n_i = pl.program_id(0), pl.num_programs(0)
    edge = jnp.dot(norm_mod(jnp.concatenate([xp_ref[0], xn_ref[0]], axis=0)), w_ref[:, 0:hy_w],
                   preferred_element_type=F32)
    before = jnp.where(i > 0, edge[SUBLANES - 1:SUBLANES, :], 0.0)
    after = jnp.where(i < n_i - 1, edge[SUBLANES:SUBLANES + 1, :], 0.0)
    uh = u[:, 0:hy_w]
    row = lax.broadcasted_iota(jnp.int32, uh.shape, 0)
    prev = jnp.where(row == 0, before, pltpu.roll(uh, 1, 0))
    nxt = jnp.where(row == tm - 1, after, pltpu.roll(uh, tm - 1, 0))
    y = prev * cw_ref[0:1, :] + uh * cw_ref[1:2, :] + nxt * cw_ref[2:3, :] + cb_ref[...]
    for p in range(hy_w // HY_W):
        ut_ref[p, 0] = y[:, HY_W * p: HY_W * (p + 1)].T

    e2 = e2_ref[...]

    def head_norm(t, g):
        sq = t * t
        hi = sq.astype(BF16)
        lo = (sq - hi.astype(F32)).astype(BF16)
        ss = (jnp.dot(hi, e2, preferred_element_type=F32) + jnp.dot(lo, e2, preferred_element_type=F32))
        return t * lax.rsqrt(ss * (1.0 / GQA_HEAD_DIM) + NORM_EPS) * g

    for i in range(GQA_HEADS * GQA_HEAD_DIM // LANES):
        t = u[:, IN_OFF[1] + LANES * i: IN_OFF[1] + LANES * (i + 1)]
        t = _rope(head_norm(t, gq_ref[...]), cg_ref, s1g_ref, s2g_ref, GQA_HEAD_DIM // 4)
        q_ref[0, :, LANES * i: LANES * (i + 1)] = (t * q_scale).astype(BF16)
    t = u[:, IN_OFF[2]:IN_OFF[3]]
    k_ref[0] = _rope(head_norm(t, gk_ref[...]), cg_ref, s1g_ref, s2g_ref, GQA_HEAD_DIM // 4).astype(BF16)
    vt_ref[0] = u[:, IN_OFF[3]:IN_OFF[4]].T.astype(BF16)

    mq = u[:, IN_OFF[4]:IN_OFF[5]]
    mqn = mq * lax.rsqrt(jnp.mean(mq * mq, axis=-1, keepdims=True) + NORM_EPS) * gmq_ref[...]
    qm = jnp.dot(mqn.astype(BF16), wq_ref[...], preferred_element_type=F32)
    n_nope = MLA_HEADS * MLA_NOPE_DIM
    qabs = jnp.dot(qm[:, :n_nope].astype(BF16), wabs_ref[...], preferred_element_type=F32)
    qabs_ref[0] = (qabs * qm_scale).astype(BF16)
    for i in range(qpe_ref.shape[2] // LANES):
        t = _rope(qm[:, n_nope + LANES * i: n_nope + LANES * (i + 1)], cm_ref, s1m_ref, s2m_ref, MLA_ROPE_DIM // 4)
        qpe_ref[0, :, LANES * i: LANES * (i + 1)] = (t * qm_scale).astype(BF16)

    mkv = u[:, IN_OFF[5]:IN_OFF[6]]
    kvn = mkv * lax.rsqrt(jnp.mean(mkv * mkv, axis=-1, keepdims=True) + NORM_EPS) * gmkv_ref[...]
    kpe = _rope(u[:, IN_OFF[6]:IN_OFF[6] + LANES], cm_ref, s1m_ref, s2m_ref, MLA_ROPE_DIM // 4).astype(BF16)
    kpe4 = jnp.dot(kpe, krep_ref[...], preferred_element_type=F32).astype(BF16)
    km_ref[0] = jnp.concatenate([kvn.astype(BF16), kpe4], axis=1)
    vmt_ref[0] = kvn.T.astype(BF16)


def _inproj(x, mod, lw, tabs):
    B, L, D = x.shape
    tm = min(TOKEN_TILE_IN, L)
    tab_spec = pl.BlockSpec((tm, LANES), lambda i, b: (i, 0))
    tok = lambda w: pl.BlockSpec((1, tm, w), lambda i, b: (b, i, 0))
    kern = functools.partial(_inproj_kernel,
                             q_scale=GQA_HEAD_DIM ** -0.5 * LOG2E, qm_scale=MLA_QK_DIM ** -0.5 * LOG2E)
    consts = [lw["g1"], lw["w_in"], lw["conv_w"], lw["conv_b"], lw["e2"], lw["gq"], lw["gk"], lw["gmq"], lw["gmkv"],
              lw["wq"], lw["wabs"], lw["krep"]]
    latent = lw["wabs"].shape[1] // MLA_HEADS
    out_w = [(IN_SPLITS[1], BF16, False), (IN_SPLITS[2], BF16, False),
             (IN_SPLITS[3], BF16, True), (MLA_HEADS * latent, BF16, False), (MLA_PE_LANES, BF16, False),
             (latent + LANES, BF16, False), (latent, BF16, True)]
    tok_t = lambda w: pl.BlockSpec((1, w, tm), lambda i, b: (b, 0, i))
    nparts = IN_SPLITS[0] // HY_W
    per = tm // SUBLANES
    x_before = pl.BlockSpec((1, SUBLANES, D), lambda i, b: (b, jnp.maximum(i * per - 1, 0), 0))
    x_after = pl.BlockSpec((1, SUBLANES, D), lambda i, b: (b, jnp.minimum((i + 1) * per, L // SUBLANES - 1), 0))
    return pl.pallas_call(
        kern,
        grid=(L // tm, B),
        in_specs=[tok(D), x_before, x_after, pl.BlockSpec((1, N_MOD, D), lambda i, b: (b, 0, 0))]
        + [_const_spec(a.shape) for a in consts] + [tab_spec] * len(tabs),
        out_specs=[pl.BlockSpec((nparts, 1, HY_W, tm), lambda i, b: (0, b, 0, i))]
        + [tok_t(w) if tr else tok(w) for w, _, tr in out_w],
        out_shape=[jax.ShapeDtypeStruct((nparts, B, HY_W, L), F32)]
        + [jax.ShapeDtypeStruct((B, w, L) if tr else (B, L, w), dt) for w, dt, tr in out_w],
        compiler_params=_cparams("parallel", "parallel"),
        name="inproj",
    )(x, x, x, mod, *consts, *tabs)


def _longconv_kernel(bias_ref, kf_ref, u_ref, o_ref, s0_ref, s1_ref, acc_ref, x_ref, y_ref, *, nblk):
    g = pl.program_id(0)
    B, CH = u_ref.shape[1], u_ref.shape[2]
    P = s0_ref.shape[1]
    for p in range(3):
        x_ref[p] = jnp.swapaxes(u_ref[p], 0, 1)

    def to_rows(z):
        return jnp.concatenate([z[:, j * CONV_BLK:(j + 1) * CONV_BLK] for j in range(nblk)], axis=0)

    def conv(zin, o, c, s_ref):
        krow = kf_ref[o, pl.ds(c, 1), :]
        s_ref[...] = pltpu.roll(jnp.broadcast_to(krow, (CONV_BLK, P)), 0, 1,
                                stride=1, stride_axis=0).astype(BF16)
        zb = zin.astype(BF16)
        acc_ref[...] = jnp.zeros_like(acc_ref)
        for d in range(-(nblk - 1), nblk):
            j0 = max(0, -d)
            n = nblk - abs(d)
            i0 = j0 + d
            col = (d * CONV_BLK) % P
            acc_ref[B * i0: B * (i0 + n), :] += jnp.dot(
                zb[B * j0: B * (j0 + n), :], s_ref[:, col: col + CONV_BLK],
                preferred_element_type=F32)
        return acc_ref[...] + zin * bias_ref[o, g * CH + c]

    def channel(c, carry):
        z1 = to_rows(x_ref[1, c]) * conv(to_rows(x_ref[0, c]), 0, c, s0_ref)
        y = to_rows(x_ref[2, c]) * conv(z1, 1, c, s1_ref)
        y_ref[c] = jnp.concatenate([y[B * i: B * (i + 1)] for i in range(nblk)], axis=1)
        return carry

    lax.fori_loop(0, CH, channel, 0, unroll=True)
    o_ref[...] = jnp.swapaxes(y_ref[...], 0, 1)


def _longconv(ut, kf, hy_bias):
    _, B, C, L = ut.shape
    P = kf.shape[-1]
    nblk = L // CONV_BLK
    return pl.pallas_call(
        functools.partial(_longconv_kernel, nblk=nblk),
        grid=(C // CONV_CH,),
        in_specs=[
            pl.BlockSpec(memory_space=pltpu.SMEM),
            pl.BlockSpec((HY_ORDER, CONV_CH, P), lambda g: (0, g, 0)),
            pl.BlockSpec((3, B, CONV_CH, L), lambda g: (0, 0, g, 0)),
        ],
        out_specs=pl.BlockSpec((B, CONV_CH, L), lambda g: (0, g, 0)),
        out_shape=jax.ShapeDtypeStruct((B, C, L), F32),
        scratch_shapes=[pltpu.VMEM((CONV_BLK, P), BF16), pltpu.VMEM((CONV_BLK, P), BF16),
                        pltpu.VMEM((nblk * B, CONV_BLK), F32),
                        pltpu.VMEM((3, CONV_CH, B, L), F32), pltpu.VMEM((CONV_CH, B, L), F32)],
        compiler_params=_cparams("parallel"),
        name="hyena_longconv",
    )(hy_bias, kf, ut)


def _attn_units_step(qs, k_refs, vt_refs, s_new, s_old, m_new, m_old, koffs, voffs, vrows, tk):
    nu = len(qs)
    M = qs[0].shape[0]
    dn = (((1,), (1,)), ((), ()))
    row = 0
    for k_ref in k_refs:
        n = k_ref.shape[1]
        for u in range(nu):
            s_new[u, row:row + n, :] = lax.dot_general(k_ref[0, :, koffs[u]:koffs[u] + qs[u].shape[1]], qs[u], dn,
                                                       preferred_element_type=F32)
        row += n
    mo = [m_old[u, 0:1, :] for u in range(nu)]
    mx = [jnp.full((SUBLANES, M), -jnp.inf, F32) for _ in range(nu)]
    ones = jnp.ones((BF16_ROWS, tk), BF16)
    acc = [jnp.zeros((vrows + BF16_ROWS, M), F32) for _ in range(nu)]
    row = 0
    for vt_ref in vt_refs:
        for c in range(vt_ref.shape[2] // tk):
            rows = slice(row, row + tk)
            row += tk
            for u in range(nu):
                p = jnp.exp2(s_old[u, rows, :] - mo[u]).astype(BF16)
                vt = jnp.concatenate([vt_ref[0, voffs[u]:voffs[u] + vrows, c * tk:(c + 1) * tk], ones], axis=0)
                acc[u] = acc[u] + jnp.dot(vt, p, preferred_element_type=F32)
                sn = s_new[u, rows, :]
                for r in range(tk // SUBLANES):
                    mx[u] = jnp.maximum(mx[u], sn[SUBLANES * r: SUBLANES * (r + 1), :])
    for u in range(nu):
        m_new[u] = jnp.broadcast_to(jnp.max(mx[u], axis=0, keepdims=True), (SUBLANES, M))
    return [a[:vrows, :] for a in acc], [a[vrows:vrows + 1, :] for a in acc]


def _attn_pipeline(step, s0_ref, s1_ref, m0_ref, m1_ref, e0_ref, e1_ref):
    g = pl.program_id(0)

    @pl.when(g == 0)
    def _():
        s1_ref[...] = jnp.zeros_like(s1_ref)
        m1_ref[...] = jnp.zeros_like(m1_ref)
        e1_ref[...] = jnp.ones_like(e1_ref)

    @pl.when(g % 2 == 0)
    def _():
        step(s0_ref, s1_ref, m0_ref, m1_ref, e0_ref, e1_ref)

    @pl.when(g % 2 == 1)
    def _():
        step(s1_ref, s0_ref, m1_ref, m0_ref, e1_ref, e0_ref)


def _attn_tile_maps(ntiles):
    scored = lambda g: jnp.minimum(g, ntiles - 1)
    valued = lambda g: jnp.clip(g - 1, 0, ntiles - 1)
    written = lambda g: jnp.maximum(g - 2, 0)
    return scored, valued, written


def _park(e_ref, acc, den):
    rows = acc.shape[0]
    e_ref[0:rows, :] = acc
    e_ref[rows:rows + SUBLANES, :] = jnp.broadcast_to(den, (SUBLANES, den.shape[1]))


def _gqa_kernel(*refs, nseg, tk):
    q_ref, k_refs, v_refs = refs[0], refs[1:1 + nseg], refs[1 + nseg:1 + 2 * nseg]
    o_ref, s0_ref, s1_ref, m0_ref, m1_ref, e0_ref, e1_ref = refs[1 + 2 * nseg:]
    tq = q_ref.shape[1]
    ntile = q_ref.shape[2] // LANES
    vrows = GQA_KV_HEADS * GQA_HEAD_DIM

    def step(s_new, s_old, m_new, m_old, e_new, e_old):
        o = e_old[0:vrows, :] * (1.0 / e_old[vrows:vrows + 1, :])
        top = lax.broadcasted_iota(jnp.int32, (vrows, tq), 0) < GQA_HEAD_DIM
        for t in range(ntile):
            oa = o[:, (2 * t) * tq:(2 * t + 1) * tq]
            ob = o[:, (2 * t + 1) * tq:(2 * t + 2) * tq]
            o_ref[0, :, LANES * t: LANES * (t + 1)] = jnp.where(top, oa, ob).T.astype(BF16)

        low = lax.broadcasted_iota(jnp.int32, (tq, LANES), 1) < GQA_HEAD_DIM
        zero = jnp.zeros((tq, LANES), BF16)
        qs = []
        for t in range(ntile):
            qt = q_ref[0, :, LANES * t: LANES * (t + 1)]
            qs.append(jnp.where(low, qt, zero))
            qs.append(jnp.where(low, zero, qt))
        (acc,), (den,) = _attn_units_step([jnp.concatenate(qs, axis=0)], k_refs, v_refs, s_new, s_old,
                                          m_new, m_old, (0,), (0,), vrows, tk)
        _park(e_new, acc, den)

    _attn_pipeline(step, s0_ref, s1_ref, m0_ref, m1_ref, e0_ref, e1_ref)


def _gqa_attention(q, ks, vts):
    B, L, W = q.shape
    T = sum(k.shape[1] for k in ks)
    tq = min(QUERY_TILE, L)
    nq = L // tq
    M = GQA_HEADS * tq
    scored, valued, written = _attn_tile_maps(B * nq)
    erows = GQA_KV_HEADS * GQA_HEAD_DIM + SUBLANES
    return pl.pallas_call(
        functools.partial(_gqa_kernel, nseg=len(ks), tk=KEY_CHUNK),
        grid=(B * nq + 2,),
        in_specs=[pl.BlockSpec((1, tq, W), lambda g: (scored(g) // nq, scored(g) % nq, 0))]
        + [pl.BlockSpec((1, k.shape[1], LANES), lambda g: (scored(g) // nq, 0, 0)) for k in ks]
        + [pl.BlockSpec((1, LANES, v.shape[2]), lambda g: (valued(g) // nq, 0, 0)) for v in vts],
        out_specs=pl.BlockSpec((1, tq, W), lambda g: (written(g) // nq, written(g) % nq, 0)),
        out_shape=jax.ShapeDtypeStruct((B, L, W), BF16),
        scratch_shapes=[pltpu.VMEM((1, T, M), F32), pltpu.VMEM((1, T, M), F32),
                        pltpu.VMEM((1, SUBLANES, M), F32), pltpu.VMEM((1, SUBLANES, M), F32),
                        pltpu.VMEM((erows, M), F32), pltpu.VMEM((erows, M), F32)],
        compiler_params=_cparams("arbitrary"),
        name="gqa_attention",
    )(q, *ks, *vts)


def _mla_kernel(*refs, nseg, tk):
    qabs_ref, qpe_ref, wvt_ref = refs[0:3]
    k_refs, v_refs = refs[3:3 + nseg], refs[3 + nseg:3 + 2 * nseg]
    o_ref, s0_ref, s1_ref, m0_ref, m1_ref, e0_ref, e1_ref = refs[3 + 2 * nseg:]
    tq = qabs_ref.shape[1]
    latent = v_refs[0].shape[1]
    slots = LANES // MLA_ROPE_DIM

    def step(s_new, s_old, m_new, m_old, e_new, e_old):
        olat = (e_old[0:latent, :] * (1.0 / e_old[latent:latent + 1, :])).astype(BF16)
        outs = [jnp.dot(wvt_ref[MLA_V_DIM * h: MLA_V_DIM * (h + 1), :], olat[:, tq * h: tq * (h + 1)],
                        preferred_element_type=F32) for h in range(MLA_HEADS)]
        for t in range(MLA_HEADS // 2):
            o_ref[0, :, LANES * t: LANES * (t + 1)] = jnp.concatenate(outs[2 * t: 2 * t + 2], axis=0).T.astype(BF16)

        lane = lax.broadcasted_iota(jnp.int32, (tq, LANES), 1)
        zero = jnp.zeros((tq, LANES), BF16)
        qs = []
        for h in range(MLA_HEADS):
            pe = qpe_ref[0, :, LANES * (h // slots): LANES * (h // slots + 1)]
            mine = (lane >= MLA_ROPE_DIM * (h % slots)) & (lane < MLA_ROPE_DIM * (h % slots + 1))
            qs.append(jnp.concatenate([qabs_ref[0, :, latent * h: latent * (h + 1)], jnp.where(mine, pe, zero)], axis=1))
        (acc,), (den,) = _attn_units_step([jnp.concatenate(qs, axis=0)], k_refs, v_refs, s_new, s_old,
                                          m_new, m_old, (0,), (0,), latent, tk)
        _park(e_new, acc, den)

    _attn_pipeline(step, s0_ref, s1_ref, m0_ref, m1_ref, e0_ref, e1_ref)


def _mla_attention(qabs, qpe, wvt, ks, vts):
    B, L, _ = qabs.shape
    T = sum(k.shape[1] for k in ks)
    tq = min(QUERY_TILE, L)
    nq = L // tq
    M = MLA_HEADS * tq
    scored, valued, written = _attn_tile_maps(B * nq)
    W = MLA_HEADS * MLA_V_DIM
    erows = vts[0].shape[1] + SUBLANES
    return pl.pallas_call(
        functools.partial(_mla_kernel, nseg=len(ks), tk=KEY_CHUNK),
        grid=(B * nq + 2,),
        in_specs=[pl.BlockSpec((1, tq, qabs.shape[2]), lambda g: (scored(g) // nq, scored(g) % nq, 0)),
                  pl.BlockSpec((1, tq, qpe.shape[2]), lambda g: (scored(g) // nq, scored(g) % nq, 0)),
                  _const_spec(wvt.shape)]
        + [pl.BlockSpec((1,) + k.shape[1:], lambda g: (scored(g) // nq, 0, 0)) for k in ks]
        + [pl.BlockSpec((1,) + v.shape[1:], lambda g: (valued(g) // nq, 0, 0)) for v in vts],
        out_specs=pl.BlockSpec((1, tq, W), lambda g: (written(g) // nq, written(g) % nq, 0)),
        out_shape=jax.ShapeDtypeStruct((B, L, W), BF16),
        scratch_shapes=[pltpu.VMEM((1, T, M), F32), pltpu.VMEM((1, T, M), F32),
                        pltpu.VMEM((1, SUBLANES, M), F32), pltpu.VMEM((1, SUBLANES, M), F32),
                        pltpu.VMEM((erows, M), F32), pltpu.VMEM((erows, M), F32)],
        compiler_params=_cparams("arbitrary"),
        name="mla_attention",
    )(qabs, qpe, wvt, *ks, *vts)


def _outffn_kernel(x_ref, yh_ref, yg_ref, ym_ref, mod_ref, g2_ref, wo_ref, w1_ref, w3_ref, w2_ref,
                   gf_ref, o_ref, *, hchunk, final):
    x = x_ref[0]
    mix = jnp.concatenate([yh_ref[0].T.astype(BF16), yg_ref[0], ym_ref[0]], axis=1)
    y = jnp.dot(mix, wo_ref[...], preferred_element_type=F32)
    x1 = x + mod_ref[0, 2:3, :] * y
    h = x1 * lax.rsqrt(jnp.mean(x1 * x1, axis=-1, keepdims=True) + NORM_EPS) * g2_ref[...]
    h = (h * (1.0 + mod_ref[0, 4:5, :]) + mod_ref[0, 3:4, :]).astype(BF16)
    f = jnp.zeros_like(x)
    hidden = w1_ref.shape[1]
    for lo in range(0, hidden, hchunk):
        cols = slice(lo, min(lo + hchunk, hidden))
        a = jnp.dot(h, w1_ref[:, cols], preferred_element_type=F32)
        b = jnp.dot(h, w3_ref[:, cols], preferred_element_type=F32)
        g = (a * jax.nn.sigmoid(a) * b).astype(BF16)
        f = f + jnp.dot(g, w2_ref[cols, :], preferred_element_type=F32)
    x2 = x1 + mod_ref[0, 5:6, :] * f
    if final:
        x2 = x2 * lax.rsqrt(jnp.mean(x2 * x2, axis=-1, keepdims=True) + NORM_EPS) * gf_ref[...]
    o_ref[0] = x2


def _outffn(x, yh_t, yg, ym, mod, lw, final_g, final):
    B, L, D = x.shape
    tm = min(TOKEN_TILE_OUT, L)
    hidden = lw["w1"].shape[1]
    consts = [lw["g2"], lw["w_out"], lw["w1"], lw["w3"], lw["w2"], final_g]
    kern = functools.partial(_outffn_kernel, hchunk=FFN_HIDDEN_CHUNK, final=final)
    tok = lambda w: pl.BlockSpec((1, tm, w), lambda b, i: (b, i, 0))
    return pl.pallas_call(
        kern,
        grid=(B, L // tm),
        in_specs=[tok(D), pl.BlockSpec((1, HY_W, tm), lambda b, i: (b, 0, i)),
                  tok(yg.shape[2]), tok(ym.shape[2]),
                  pl.BlockSpec((1, N_MOD, D), lambda b, i: (b, 0, 0))]
        + [_const_spec(a.shape) for a in consts],
        out_specs=tok(D),
        out_shape=jax.ShapeDtypeStruct((B, L, D), F32),
        compiler_params=_cparams("parallel", "parallel"),
        name="outffn",
    )(x, yh_t, yg, ym, mod, *consts)


def _rope_tables(L, with_pos):
    lane = np.arange(LANES)

    def pattern(dim, off, width):
        loc = (lane - off) % dim
        half = dim // 2
        active = (lane >= off) & (lane < off + width)
        use_col = loc >= half
        fi = (loc % half) % (half // 2)
        first = (loc % half) < (half // 2)
        return active, use_col, fi, first, half

    def tables(dim, off, width):
        active, use_col, fi, first, half = pattern(dim, off, width)
        if not with_pos:
            return [np.ones((L, LANES), np.float32), np.zeros((L, LANES), np.float32),
                    np.zeros((L, LANES), np.float32)]
        row = np.repeat(np.arange(L // GRID_W), GRID_W).astype(np.float32)
        col = np.tile(np.arange(GRID_W), L // GRID_W).astype(np.float32)
        inv = (np.float32(ROPE_THETA) ** (-np.arange(0, half, 2, dtype=np.float32) / np.float32(half))).astype(np.float32)
        pos = np.where(use_col[None, :], col[:, None], row[:, None])
        ang = (pos * inv[fi][None, :]).astype(np.float32).astype(np.float64)
        act = active[None, :]
        fst = first[None, :]
        cos = np.where(act, np.cos(ang), 1.0).astype(np.float32)
        sin = np.sin(ang)
        s1 = np.where(act & fst, -sin, 0.0).astype(np.float32)
        s2 = np.where(act & ~fst, sin, 0.0).astype(np.float32)
        return [cos, s1, s2]

    return tables(GQA_HEAD_DIM, 0, LANES) + tables(MLA_ROPE_DIM, 0, LANES)


def _gqa_tile_order():
    per = GQA_HEADS // GQA_KV_HEADS
    return [h for t in range(per) for h in (t, per + t)]


def _layer_weights(l, p):
    hd = GQA_HEAD_DIM
    order = _gqa_tile_order()
    qperm = np.concatenate([np.arange(hd) + hd * h for h in order])
    w_in = p["w_in"][l]
    w_in = jnp.concatenate([w_in[:, :IN_OFF[1]], w_in[:, IN_OFF[1]:IN_OFF[2]][:, qperm], w_in[:, IN_OFF[2]:]], axis=1)
    w_in = jnp.pad(w_in, ((0, 0), (0, IN_WIDTH_PAD - w_in.shape[1]))).astype(BF16)

    wq = p["mla_w_uq"][l].reshape(-1, MLA_HEADS, MLA_QK_DIM)
    rank_q = wq.shape[0]
    wq_pe = wq[:, :, MLA_NOPE_DIM:].reshape(rank_q, MLA_HEADS * MLA_ROPE_DIM)
    wq = jnp.concatenate([wq[:, :, :MLA_NOPE_DIM].reshape(rank_q, MLA_HEADS * MLA_NOPE_DIM),
                          jnp.pad(wq_pe, ((0, 0), (0, MLA_PE_LANES - wq_pe.shape[1])))], axis=1).astype(BF16)

    wkv = p["mla_w_ukv"][l].reshape(-1, MLA_HEADS, MLA_NOPE_DIM + MLA_V_DIM)
    latent = wkv.shape[0]
    wk_t = jnp.transpose(wkv[:, :, :MLA_NOPE_DIM], (1, 2, 0))
    eye = jnp.asarray(np.eye(MLA_HEADS, dtype=np.float32))
    wabs = (wk_t[:, :, None, :] * eye[:, None, :, None]).reshape(MLA_HEADS * MLA_NOPE_DIM, MLA_HEADS * latent)
    wvt = jnp.transpose(wkv[:, :, MLA_NOPE_DIM:], (1, 2, 0)).reshape(MLA_HEADS * MLA_V_DIM, latent)
    krep = np.zeros((LANES, LANES), np.float32)
    for s in range(LANES // MLA_ROPE_DIM):
        krep[np.arange(MLA_ROPE_DIM), s * MLA_ROPE_DIM + np.arange(MLA_ROPE_DIM)] = 1.0

    w_out = p["w_out"][l]
    g0 = HY_W
    g1 = g0 + GQA_HEADS * hd
    w_out = jnp.concatenate([w_out[:g0], w_out[g0:g1][qperm], w_out[g1:]], axis=0).astype(BF16)

    e2 = np.kron(np.eye(LANES // hd, dtype=np.float32), np.ones((hd, hd), np.float32))
    two = lambda g: jnp.tile(g, LANES // hd)[None, :]
    return {
        "g1": p["norm1_g"][l][None, :], "g2": p["norm2_g"][l][None, :],
        "w_in": w_in, "e2": jnp.asarray(e2, BF16),
        "gq": two(p["gqa_q_g"][l]), "gk": two(p["gqa_k_g"][l]),
        "gmq": p["mla_q_g"][l][None, :], "gmkv": p["mla_kv_g"][l][None, :],
        "wq": wq, "wabs": wabs.astype(BF16), "wvt": wvt.astype(BF16), "krep": jnp.asarray(krep, BF16),
        "w_out": w_out,
        "w1": p["ffn_w1"][l].astype(BF16), "w3": p["ffn_w3"][l].astype(BF16), "w2": p["ffn_w2"][l].astype(BF16),
        "conv_w": p["hy_conv_w"][l], "conv_b": p["hy_conv_b"][l][None, :], "hy_bias": p["hy_bias"][l],
    }


def kernel(x, c, ctx, c_ctx, mod_w, mod_b, norm1_g, norm2_g, w_in, hy_conv_w, hy_conv_b, hy_filt_w1, hy_filt_b1, hy_filt_w2, hy_filt_b2, hy_filt_w3, hy_filt_freq, hy_bias, gqa_q_g, gqa_k_g, mla_q_g, mla_kv_g, mla_w_uq, mla_w_ukv, w_out, ffn_w1, ffn_w3, ffn_w2, final_g):
    p = dict(norm1_g=norm1_g, norm2_g=norm2_g, w_in=w_in, hy_conv_w=hy_conv_w, hy_conv_b=hy_conv_b,
             hy_bias=hy_bias, gqa_q_g=gqa_q_g, gqa_k_g=gqa_k_g, mla_q_g=mla_q_g, mla_kv_g=mla_kv_g,
             mla_w_uq=mla_w_uq, mla_w_ukv=mla_w_ukv, w_out=w_out, ffn_w1=ffn_w1, ffn_w3=ffn_w3, ffn_w2=ffn_w2)
    B, L, D = x.shape
    Lc = ctx.shape[1]
    depth = mod_w.shape[0]

    pad_rows = (-(B + 1)) % 8
    c_all = jnp.concatenate([c, c_ctx[None, :], jnp.zeros((pad_rows, D), F32)], axis=0)
    mod = _modulation(c_all, mod_w, mod_b).reshape(depth, -1, N_MOD, D)

    tabs_lat = _rope_tables(L, True)
    tabs_ctx = _rope_tables(Lc, False)
    final_g2 = final_g[None, :]

    x_lat, x_ctx = x, ctx
    for l in range(depth):
        last = l == depth - 1
        lw = _layer_weights(l, p)
        filt = (hy_filt_w1[l], hy_filt_b1[l], hy_filt_w2[l], hy_filt_b2[l], hy_filt_w3[l], hy_filt_freq[l])
        mod_lat = mod[l, :B]
        mod_ctx = jnp.broadcast_to(mod[l, B][None], (B, N_MOD, D))

        ut_c, q_c, k_c, vt_c, qa_c, qp_c, km_c, vmt_c = _inproj(x_ctx, mod_ctx, lw, tabs_ctx)
        ut, q, k, vt, qa, qp, km, vmt = _inproj(x_lat, mod_lat, lw, tabs_lat)

        if not last:
            yh_c = _longconv(ut_c, _hyena_filters(Lc, *filt), lw["hy_bias"])
            yg_c = _gqa_attention(q_c, [k_c], [vt_c])
            ym_c = _mla_attention(qa_c, qp_c, lw["wvt"], [km_c], [vmt_c])
            x_ctx_next = _outffn(x_ctx, yh_c, yg_c, ym_c, mod_ctx, lw, final_g2, False)

        yh = _longconv(ut, _hyena_filters(L, *filt), lw["hy_bias"])
        yg = _gqa_attention(q, [k, k_c], [vt, vt_c])
        ym = _mla_attention(qa, qp, lw["wvt"], [km, km_c], [vmt, vmt_c])
        x_lat = _outffn(x_lat, yh, yg, ym, mod_lat, lw, final_g2, last)
        if not last:
            x_ctx = x_ctx_next
    return x_lat
```
